```python
import math
import jax, jax.numpy as jnp
from jax import lax
import numpy as np

D_MODEL = 1024
BATCH = 8
SEQ = 4096
DEPTH = 2

HEAD_DIM = 64
NSA_HEADS = 8
NSA_KV_HEADS = 2
NSA_WIDTH = NSA_HEADS * HEAD_DIM
NSA_KV_WIDTH = NSA_KV_HEADS * HEAD_DIM
CMP_BLOCK = 32
CMP_STRIDE = 16
CMP_HIDDEN = 256
SEL_BLOCK = 64
SEL_TOPK = 16
WINDOW = 512
NSA_Q_BLOCK = 64
DIFF_HEADS = 4
DIFF_QK_WIDTH = 2 * DIFF_HEADS * HEAD_DIM
DIFF_V_WIDTH = DIFF_HEADS * 2 * HEAD_DIM
DIFF_Q_BLOCK = 128
D_FF = -(-8 * D_MODEL // (3 * 256)) * 256
ROPE_THETA = 500000.0
ROPE_FRACTION = 4
ALPHA = (2 * DEPTH) ** 0.25
BETA = (8 * DEPTH) ** -0.25
EPS = 1e-5
NEG_INF = -1e30
FORCE_SCORE = 1e9
IN_SIZES = (NSA_WIDTH, NSA_KV_WIDTH, NSA_KV_WIDTH, NSA_KV_WIDTH, NSA_KV_WIDTH,
            NSA_KV_WIDTH, NSA_KV_WIDTH, NSA_HEADS * 3,
            DIFF_QK_WIDTH, DIFF_QK_WIDTH, DIFF_V_WIDTH, 2 * D_MODEL)
IN_COLS = sum(IN_SIZES)

kernel_name = "hybrid_nsa_diffattn_deepnorm_adaln"


def layer_norm(x, g, b):
    x32 = x.astype(jnp.float32)
    mu = jnp.mean(x32, -1, keepdims=True)
    var = jnp.mean(jnp.square(x32 - mu), -1, keepdims=True)
    return ((x32 - mu) * lax.rsqrt(var + EPS) * g + b).astype(x.dtype)


def rope_partial(t, pos):
    rot = t.shape[-1] // ROPE_FRACTION
    half = rot // 2
    inv_freq = ROPE_THETA ** (-jnp.arange(half, dtype=jnp.float32) * 2.0 / rot)
    ang = pos[:, None] * inv_freq[None, :]
    cos = jnp.cos(ang)[:, None, :]
    sin = jnp.sin(ang)[:, None, :]
    t1 = t[..., :half].astype(jnp.float32)
    t2 = t[..., half:rot].astype(jnp.float32)
    r1 = (t1 * cos - t2 * sin).astype(t.dtype)
    r2 = (t2 * cos + t1 * sin).astype(t.dtype)
    return jnp.concatenate([r1, r2, t[..., rot:]], axis=-1)


def masked_softmax(s, mask):
    return jax.nn.softmax(jnp.where(mask, s.astype(jnp.float32), NEG_INF), axis=-1)


def compress(k, pe, w1, w2):
    B, S, G, Dh = k.shape
    nc = (S - CMP_BLOCK) // CMP_STRIDE + 1
    idx = jnp.arange(nc)[:, None] * CMP_STRIDE + jnp.arange(CMP_BLOCK)[None, :]
    blk = k[:, idx] + pe[None, None, :, None, :]
    flat = blk.transpose(0, 1, 3, 2, 4).reshape(B, nc, G, CMP_BLOCK * Dh)
    return jax.nn.gelu(flat @ w1) @ w2


def cmp_to_sel_weights(nc, nb):
    cs = np.arange(nc) * CMP_STRIDE
    bs = np.arange(nb) * SEL_BLOCK
    ov = np.minimum(cs[:, None] + CMP_BLOCK, bs[None, :] + SEL_BLOCK) - np.maximum(cs[:, None], bs[None, :])
    return jnp.asarray(np.clip(ov, 0, None) / CMP_BLOCK, dtype=jnp.float32)


def nsa_attention(q, kc_raw, vc_raw, ks, vs, kw, vw, gates, pe_k, w1_k, w2_k, pe_v, w1_v, w2_v, pos):
    B, S, H, Dh = q.shape
    G = ks.shape[2]
    R = H // G
    C = NSA_Q_BLOCK
    nb = S // SEL_BLOCK
    topk = min(SEL_TOPK, nb)
    NK = topk * SEL_BLOCK
    scale = Dh ** -0.5
    q_rot = rope_partial(q, pos)
    ks = rope_partial(ks, pos)
    kw = rope_partial(kw, pos)
    k_cmp = compress(kc_raw, pe_k, w1_k, w2_k)
    v_cmp = compress(vc_raw, pe_v, w1_v, w2_v)
    nc = k_cmp.shape[1]
    cmp_end = jnp.arange(nc) * CMP_STRIDE + CMP_BLOCK - 1
    ov = cmp_to_sel_weights(nc, nb)
    ks_t = ks.transpose(0, 2, 1, 3)
    vs_t = vs.transpose(0, 2, 1, 3)
    pad = ((0, 0), (WINDOW, 0), (0, 0), (0, 0))
    kw_pad = jnp.pad(kw, pad)
    vw_pad = jnp.pad(vw, pad)
    blk_ids = jnp.arange(nb)
    b_idx = jnp.arange(B)[:, None, None]
    g_idx = jnp.arange(G)[None, :, None]

    def one_block(i):
        t0 = i * C
        t = t0 + jnp.arange(C)
        qb = lax.dynamic_slice_in_dim(q, t0, C, 1).reshape(B, C, G, R, Dh)
        qrb = lax.dynamic_slice_in_dim(q_rot, t0, C, 1).reshape(B, C, G, R, Dh)
        gb = lax.dynamic_slice_in_dim(gates, t0, C, 1).reshape(B, C, G, R, 3)
        valid_c = cmp_end[None, :] <= t[:, None]
        s = jnp.einsum('bcgrd,bngd->bcgrn', qb, k_cmp) * scale
        p_cmp = masked_softmax(s, valid_c[None, :, None, None, :])
        p_cmp = p_cmp * jnp.any(valid_c, -1)[None, :, None, None, None].astype(jnp.float32)
        o_cmp = jnp.einsum('bcgrn,bngd->bcgrd', p_cmp.astype(v_cmp.dtype), v_cmp)
        imp = jnp.einsum('bcgrn,nj->bcgj', p_cmp, ov)
        cur = t // SEL_BLOCK
        future = blk_ids[None, :] > cur[:, None]
        forced = (blk_ids[None, :] == 0) | (blk_ids[None, :] == cur[:, None]) | (blk_ids[None, :] == cur[:, None] - 1)
        imp = jnp.where(future[None, :, None, :], -jnp.inf,
                        jnp.where(forced[None, :, None, :], FORCE_SCORE, imp))
        _, sel = lax.top_k(imp, topk)
        tok = (sel[..., None] * SEL_BLOCK + jnp.arange(SEL_BLOCK)).reshape(B, C, G, NK)
        tok_t = tok.transpose(0, 2, 1, 3).reshape(B, G, C * NK)
        k_g = ks_t[b_idx, g_idx, tok_t].reshape(B, G, C, NK, Dh)
        v_g = vs_t[b_idx, g_idx, tok_t].reshape(B, G, C, NK, Dh)
        s = jnp.einsum('bcgrd,bgcnd->bcgrn', qrb, k_g) * scale
        p = masked_softmax(s, (tok <= t[None, :, None, None])[:, :, :, None, :])
        o_sel = jnp.einsum('bcgrn,bgcnd->bcgrd', p.astype(v_g.dtype), v_g)
        kb = lax.dynamic_slice_in_dim(kw_pad, t0, WINDOW + C, 1)
        vb = lax.dynamic_slice_in_dim(vw_pad, t0, WINDOW + C, 1)
        spos = t0 - WINDOW + jnp.arange(WINDOW + C)
        valid_w = (spos[None, :] <= t[:, None]) & (spos[None, :] > t[:, None] - WINDOW) & (spos[None, :] >= 0)
        s = jnp.einsum('bcgrd,bngd->bcgrn', qrb, kb) * scale
        p = masked_softmax(s, valid_w[None, :, None, None, :])
        o_win = jnp.einsum('bcgrn,bngd->bcgrd', p.astype(vb.dtype), vb)
        out = gb[..., 0:1] * o_cmp + gb[..., 1:2] * o_sel + gb[..., 2:3] * o_win
        return out.reshape(B, C, H * Dh)

    out = lax.map(one_block, jnp.arange(S // C))
    return out.transpose(1, 0, 2, 3).reshape(B, S, H * Dh)


def diff_attention(q, k, v, lam_vec, norm_g, lambda_init, pos):
    B, S, _, Dh = q.shape
    Hd = v.shape[2]
    QB = DIFF_Q_BLOCK
    scale = Dh ** -0.5
    q = rope_partial(q, pos).reshape(B, S, Hd, 2, Dh)
    k = rope_partial(k, pos).reshape(B, S, Hd, 2, Dh)
    lam32 = lam_vec.astype(jnp.float32)
    lam = jnp.exp(jnp.sum(lam32[0] * lam32[1])) - jnp.exp(jnp.sum(lam32[2] * lam32[3])) + lambda_init
    kpos = jnp.arange(S)

    def one_block(i):
        t0 = i * QB
        qb = lax.dynamic_slice_in_dim(q, t0, QB, 1)
        s = jnp.einsum('bqhmd,bkhmd->bhmqk', qb, k) * scale
        mask = kpos[None, :] <= (t0 + jnp.arange(QB))[:, None]
        p = masked_softmax(s, mask[None, None, None])
        a = p[:, :, 0] - lam * p[:, :, 1]
        return jnp.einsum('bhqk,bkhe->bqhe', a.astype(v.dtype), v)

    o = lax.map(one_block, jnp.arange(S // QB))
    o = o.transpose(1, 0, 2, 3, 4).reshape(B, S, Hd, 2 * Dh).astype(jnp.float32)
    o = o * lax.rsqrt(jnp.mean(jnp.square(o), -1, keepdims=True) + EPS) * norm_g * (1.0 - lambda_init)
    return o.reshape(B, S, Hd * 2 * Dh).astype(v.dtype)


def token_mixer(u, w_in, pe_k, w1_k, w2_k, pe_v, w1_v, w2_v, lam_vec, norm_g,
                w_bn, w_bd, w_o, lambda_init):
    B, S, _ = u.shape
    offs = np.cumsum(IN_SIZES)[:-1].tolist()
    (q_n, kc, vc, ksl, vsl, kwn, vwn, g_n, q_d, k_d, v_d, g_m) = jnp.split(u @ w_in, offs, axis=-1)
    pos = jnp.arange(S, dtype=jnp.float32)
    kvh = lambda t: t.reshape(B, S, NSA_KV_HEADS, HEAD_DIM)
    y_nsa = nsa_attention(q_n.reshape(B, S, NSA_HEADS, HEAD_DIM), kvh(kc), kvh(vc), kvh(ksl), kvh(vsl),
                          kvh(kwn), kvh(vwn), jax.nn.sigmoid(g_n).reshape(B, S, NSA_HEADS, 3),
                          pe_k, w1_k, w2_k, pe_v, w1_v, w2_v, pos) @ w_bn
    y_diff = diff_attention(q_d.reshape(B, S, 2 * DIFF_HEADS, HEAD_DIM),
                            k_d.reshape(B, S, 2 * DIFF_HEADS, HEAD_DIM),
                            v_d.reshape(B, S, DIFF_HEADS, 2 * HEAD_DIM),
                            lam_vec, norm_g, lambda_init, pos) @ w_bd
    g_a, g_b = jnp.split(jax.nn.sigmoid(g_m), 2, axis=-1)
    return (g_a * y_nsa + g_b * y_diff) @ w_o


def swiglu(u, w_in, w_out):
    a, b = jnp.split(u @ w_in, 2, axis=-1)
    return (jax.nn.silu(a) * b) @ w_out


def setup_inputs(seed: int = 0) -> dict:
    key = jax.random.key(seed)
    ks = jax.random.split(key, 24)
    nrm = lambda k, shape, s: jax.random.normal(k, shape, jnp.float32) * s
    L = DEPTH
    return {
        "x": nrm(ks[0], (BATCH, SEQ, D_MODEL), 1.0),
        "c": nrm(ks[1], (BATCH, D_MODEL), 1.0),
        "w_ada": nrm(ks[2], (L, D_MODEL, 6 * D_MODEL), 0.1 * D_MODEL ** -0.5),
        "b_ada": nrm(ks[3], (L, 6 * D_MODEL), 0.01),
        "w_in": nrm(ks[4], (L, D_MODEL, IN_COLS), D_MODEL ** -0.5),
        "cmp_pe_k": nrm(ks[5], (L, CMP_BLOCK, HEAD_DIM), 0.1),
        "cmp_w1_k": nrm(ks[6], (L, CMP_BLOCK * HEAD_DIM, CMP_HIDDEN), (CMP_BLOCK * HEAD_DIM) ** -0.5),
        "cmp_w2_k": nrm(ks[7], (L, CMP_HIDDEN, HEAD_DIM), CMP_HIDDEN ** -0.5),
        "cmp_pe_v": nrm(ks[8], (L, CMP_BLOCK, HEAD_DIM), 0.1),
        "cmp_w1_v": nrm(ks[9], (L, CMP_BLOCK * HEAD_DIM, CMP_HIDDEN), (CMP_BLOCK * HEAD_DIM) ** -0.5),
        "cmp_w2_v": nrm(ks[10], (L, CMP_HIDDEN, HEAD_DIM), CMP_HIDDEN ** -0.5),
        "diff_lambda": nrm(ks[11], (L, 4, HEAD_DIM), 0.1),
        "diff_norm_g": 1.0 + nrm(ks[12], (L, 2 * HEAD_DIM), 0.02),
        "w_branch_nsa": nrm(ks[13], (L, NSA_WIDTH, D_MODEL), BETA * NSA_WIDTH ** -0.5),
        "w_branch_diff": nrm(ks[14], (L, DIFF_V_WIDTH, D_MODEL), BETA * DIFF_V_WIDTH ** -0.5),
        "w_out": nrm(ks[15], (L, D_MODEL, D_MODEL), BETA * D_MODEL ** -0.5),
        "ln1_g": 1.0 + nrm(ks[16], (L, D_MODEL), 0.02),
        "ln1_b": nrm(ks[17], (L, D_MODEL), 0.01),
        "w_ffn_in": nrm(ks[18], (L, D_MODEL, 2 * D_FF), D_MODEL ** -0.5),
        "w_ffn_out": nrm(ks[19], (L, D_FF, D_MODEL), BETA * D_FF ** -0.5),
        "ln2_g": 1.0 + nrm(ks[20], (L, D_MODEL), 0.02),
        "ln2_b": nrm(ks[21], (L, D_MODEL), 0.01),
    }


def reference(x, c, w_ada, b_ada, w_in, cmp_pe_k, cmp_w1_k, cmp_w2_k, cmp_pe_v, cmp_w1_v, cmp_w2_v,
              diff_lambda, diff_norm_g, w_branch_nsa, w_branch_diff, w_out, ln1_g, ln1_b,
              w_ffn_in, w_ffn_out, ln2_g, ln2_b):
    for l in range(DEPTH):
        lambda_init = 0.8 - 0.6 * math.exp(-0.3 * l)
        ada = jax.nn.silu(c) @ w_ada[l] + b_ada[l]
        sh1, sc1, g1, sh2, sc2, g2 = jnp.split(ada[:, None, :], 6, axis=-1)
        u = x * (1.0 + sc1) + sh1
        mix = token_mixer(u, w_in[l], cmp_pe_k[l], cmp_w1_k[l], cmp_w2_k[l], cmp_pe_v[l], cmp_w1_v[l],
                          cmp_w2_v[l], diff_lambda[l], diff_norm_g[l], w_branch_nsa[l], w_branch_diff[l],
                          w_out[l], lambda_init)
        x = layer_norm(ALPHA * x + (1.0 + g1) * mix, ln1_g[l], ln1_b[l])
        u = x * (1.0 + sc2) + sh2
        x = layer_norm(ALPHA * x + (1.0 + g2) * swiglu(u, w_ffn_in[l], w_ffn_out[l]), ln2_g[l], ln2_b[l])
    return x
```

```python
import functools
import math

import numpy as np
import jax
import jax.numpy as jnp
from jax import lax
from jax.experimental import pallas as pl
from jax.experimental.pallas import tpu as pltpu

F32 = jnp.float32
BF16 = jnp.bfloat16

HEAD_DIM = 64
NSA_HEADS = 8
NSA_KV_HEADS = 2
NSA_GROUP = NSA_HEADS // NSA_KV_HEADS
CMP_BLOCK = 32
CMP_STRIDE = 16
CMP_HIDDEN = 256
SEL_BLOCK = 64
SEL_TOPK = 16
WINDOW = 512
DIFF_HEADS = 4
ROPE_THETA = 500000.0
ROPE_DIMS = HEAD_DIM // 4
EPS = 1e-5
NEG = -1e30
FORCE_SCORE = 1e9
QK_SCALE = HEAD_DIM ** -0.5

LANES = 128
VMEM_LIMIT = 56 * 1024 * 1024

PROJ_TN = 768
COL_QC = 0
COL_VD = 1024
COL_KC = 1536
COL_VC = 1664
COL_VS = 1792
COL_VW = 1920
COL_GN = 2048
COL_QD = 2304
COL_KS = 2816
COL_KW = 2944
COL_QR = 3072
COL_KD = 4096
PROJ_COLS = 4608
ROPE_TILE0 = COL_QD // PROJ_TN
GATE_TILE = COL_GN // PROJ_TN


def _nt(a, b):
    return lax.dot_general(a, b, (((1,), (1,)), ((), ())), preferred_element_type=F32)


def _split_bf16(a):
    hi = a.astype(BF16)
    lo = (a - hi.astype(F32)).astype(BF16)
    return hi, lo


def _layer_norm(h, g, b):
    mu = jnp.mean(h, axis=-1, keepdims=True)
    d = h - mu
    var = jnp.mean(d * d, axis=-1, keepdims=True)
    return d * lax.rsqrt(var + EPS) * g + b


def _ada_kernel(c_ref, w_ref, b_ref, o_ref):
    c = c_ref[...]
    a = c * jax.nn.sigmoid(c)
    a_hi, a_lo = _split_bf16(a)
    w_hi, w_lo = _split_bf16(w_ref[...])
    acc = jnp.dot(a_hi, w_hi, preferred_element_type=F32)
    acc += jnp.dot(a_lo, w_hi, preferred_element_type=F32)
    acc += jnp.dot(a_hi, w_lo, preferred_element_type=F32)
    o_ref[...] = acc + b_ref[...]


def _ada(c, w, b):
    bsz, d = c.shape
    n = w.shape[1]
    tn = 1024
    return pl.pallas_call(
        _ada_kernel,
        grid=(n // tn,),
        in_specs=[pl.BlockSpec((bsz, d), lambda j: (0, 0)),
                  pl.BlockSpec((d, tn), lambda j: (0, j)),
                  pl.BlockSpec((1, tn), lambda j: (0, j))],
        out_specs=pl.BlockSpec((bsz, tn), lambda j: (0, j)),
        out_shape=jax.ShapeDtypeStruct((bsz, n), F32),
        compiler_params=pltpu.CompilerParams(dimension_semantics=("arbitrary",),
                                             vmem_limit_bytes=VMEM_LIMIT),
        name="ada",
    )(c, w, b.reshape(1, n))


def _proj_kernel(x_ref, ada_ref, w_ref, cs_ref, cos_ref, sa_ref, sb_ref, o_ref, g_ref, u_ref, acc_ref,
                 *, row_chunk):
    j = pl.program_id(1)
    tm, tn = acc_ref.shape

    @pl.when(j == 0)
    def _():
        sh = ada_ref[0, 0:1, :]
        sc = ada_ref[0, 1:2, :]
        u_ref[...] = (x_ref[...] * (1.0 + sc) + sh).astype(BF16)

    acc_ref[...] = jnp.dot(u_ref[...], w_ref[...], preferred_element_type=F32)

    @pl.when(j < ROPE_TILE0)
    def _():
        o_ref[...] = (acc_ref[...] * cs_ref[...]).astype(BF16)

    @pl.when(j == GATE_TILE)
    def _():
        off = COL_GN - GATE_TILE * PROJ_TN
        g_ref[...] = jax.nn.sigmoid(acc_ref[:, off:off + LANES])

    @pl.when(j >= ROPE_TILE0)
    def _():
        def body(rc, carry):
            rows = pl.ds(pl.multiple_of(rc * row_chunk, row_chunk), row_chunk)
            cosv = cos_ref[rows, :]
            sav = sa_ref[rows, :]
            sbv = sb_ref[rows, :]
            for s in range(tn // LANES):
                cols = slice(s * LANES, (s + 1) * LANES)
                a = acc_ref[rows, cols] * cs_ref[:, cols]
                r = a * cosv + pltpu.roll(a, LANES - ROPE_DIMS // 2, 1) * sav + pltpu.roll(a, ROPE_DIMS // 2, 1) * sbv
                o_ref[rows, cols] = r.astype(BF16)
            return carry

        lax.fori_loop(0, tm // row_chunk, body, 0)


def _proj(x2, ada3, w, colscale, cos_t, sa_t, sb_t, seq):
    m, d = x2.shape
    tm = min(1024, seq)
    tn = PROJ_TN
    per_seq = seq // tm
    kern = functools.partial(_proj_kernel, row_chunk=128)
    return pl.pallas_call(
        kern,
        grid=(m // tm, PROJ_COLS // tn),
        in_specs=[pl.BlockSpec((tm, d), lambda i, j: (i, 0)),
                  pl.BlockSpec((1, 6, d), lambda i, j: (i // per_seq, 0, 0)),
                  pl.BlockSpec((d, tn), lambda i, j: (0, j)),
                  pl.BlockSpec((1, tn), lambda i, j: (0, j)),
                  pl.BlockSpec((tm, LANES), lambda i, j: (i % per_seq, 0)),
                  pl.BlockSpec((tm, LANES), lambda i, j: (i % per_seq, 0)),
                  pl.BlockSpec((tm, LANES), lambda i, j: (i % per_seq, 0))],
        out_specs=[pl.BlockSpec((tm, tn), lambda i, j: (i, j)),
                   pl.BlockSpec((tm, LANES), lambda i, j: (i, 0))],
        out_shape=[jax.ShapeDtypeStruct((m, PROJ_COLS), BF16),
                   jax.ShapeDtypeStruct((m, LANES), F32)],
        scratch_shapes=[pltpu.VMEM((tm, d), BF16), pltpu.VMEM((tm, tn), F32)],
        compiler_params=pltpu.CompilerParams(dimension_semantics=("arbitrary", "arbitrary"),
                                             vmem_limit_bytes=VMEM_LIMIT),
        name="proj",
    )(x2, ada3, w, colscale, cos_t, sa_t, sb_t)


def _gelu_tanh(x):
    return x * (0.5 * (1.0 + jnp.tanh(math.sqrt(2.0 / math.pi) * (x + 0.044715 * (x * x * x)))))


def _compress_kernel(xr_ref, w1_ref, pe_ref, w2_ref, o_ref):
    half = w1_ref.shape[1] // 2
    w1_top = w1_ref[0, 0:half, :]
    w1_bot = w1_ref[0, half:2 * half, :]
    pe_row = jnp.dot(pe_ref[0], w1_ref[0], preferred_element_type=F32)[0:1, :]
    nrow = xr_ref.shape[3]
    out = jnp.zeros((nrow, LANES), F32)
    for g in range(NSA_KV_HEADS):
        x = xr_ref[0, 0, g]
        a = jnp.dot(x, w1_top, preferred_element_type=F32)
        b = jnp.dot(x, w1_bot, preferred_element_type=F32)
        h = a + pltpu.roll(b, nrow - 1, 0) + pe_row
        out = out + jnp.dot(_gelu_tanh(h).astype(BF16), w2_ref[0, g], preferred_element_type=F32)
    o_ref[0, 0] = out.astype(BF16)


def _compress(xr, w1, pe, w2w):
    _, bsz, ng, nrow, width = xr.shape
    return pl.pallas_call(
        _compress_kernel,
        grid=(2, bsz),
        in_specs=[pl.BlockSpec((1, 1, ng, nrow, width), lambda w, b: (w, b, 0, 0, 0)),
                  pl.BlockSpec((1, 2 * width, CMP_HIDDEN), lambda w, b: (w, 0, 0)),
                  pl.BlockSpec((1, 8, 2 * width), lambda w, b: (w, 0, 0)),
                  pl.BlockSpec((1, ng, CMP_HIDDEN, LANES), lambda w, b: (w, 0, 0, 0))],
        out_specs=pl.BlockSpec((1, 1, nrow, LANES), lambda w, b: (w, b, 0, 0)),
        out_shape=jax.ShapeDtypeStruct((2, bsz, nrow, LANES), BF16),
        compiler_params=pltpu.CompilerParams(dimension_semantics=("arbitrary", "arbitrary"),
                                             vmem_limit_bytes=VMEM_LIMIT),
        name="compress",
    )(xr, w1, pe, w2w)


NSA_ROWS = NSA_GROUP * SEL_BLOCK
SEL_CHUNK = 256
WIN_KEYS = WINDOW + SEL_BLOCK


def _nsa_kernel(qr_ref, qc_ref, kaug_ref, vs_ref, kw_ref, vw_ref, kc_ref, vc_ref, gate_ref,
                ovt_ref, wbias_ref, tri_ref, o_ref, m_ref, l_ref, acc_ref, ocmp_ref):
    i = pl.program_id(1)
    t0 = i * SEL_BLOCK
    ncmp = kc_ref.shape[2]

    def stack_heads(ref, g):
        return jnp.concatenate(
            [ref[0, :, (g * NSA_GROUP + r) * LANES:(g * NSA_GROUP + r + 1) * LANES] for r in range(NSA_GROUP)],
            axis=0)

    row_t = t0 + (lax.broadcasted_iota(jnp.int32, (NSA_ROWS, ncmp), 0) & (SEL_BLOCK - 1))
    cmp_end = lax.broadcasted_iota(jnp.int32, (NSA_ROWS, ncmp), 1) * CMP_STRIDE + (CMP_BLOCK - 1)
    valid_c = cmp_end <= row_t
    any_valid = jnp.max(jnp.where(valid_c, 1.0, 0.0), axis=1, keepdims=True)
    psums = []
    for g in range(NSA_KV_HEADS):
        qc = stack_heads(qc_ref, g)
        s = jnp.where(valid_c, _nt(qc, kc_ref[0, 0]), NEG)
        mx = jnp.max(s, axis=1, keepdims=True)
        e = jnp.exp(s - mx)
        p = e * (any_valid / jnp.sum(e, axis=1, keepdims=True))
        ocmp_ref[g] = jnp.dot(p.astype(BF16), vc_ref[0, 0], preferred_element_type=F32)
        psums.append(p[0:SEL_BLOCK] + p[SEL_BLOCK:2 * SEL_BLOCK]
                     + p[2 * SEL_BLOCK:3 * SEL_BLOCK] + p[3 * SEL_BLOCK:4 * SEL_BLOCK])
    p_hi, p_lo = _split_bf16(jnp.concatenate(psums, axis=0))
    imp_t = _nt(ovt_ref[...], p_hi) + _nt(ovt_ref[...], p_lo)

    nb_pad = imp_t.shape[0]
    jj = lax.broadcasted_iota(jnp.int32, (nb_pad, LANES), 0)
    forced = (jj == 0) | (jj == i) | (jj == i - 1)
    val = jnp.where(jj > i, -jnp.inf, jnp.where(forced, FORCE_SCORE, imp_t))
    sub = 8
    blocks = [val[v * sub:(v + 1) * sub, :] for v in range(nb_pad // sub)]
    ranks = [jnp.zeros((sub, LANES), F32) for _ in blocks]
    jloc = lax.broadcasted_iota(jnp.int32, (sub, LANES), 0)
    for k in range(nb_pad):
        rowk = jnp.broadcast_to(val[k:k + 1, :], (sub, LANES))
        for v in range(nb_pad // sub):
            if v * sub > k:
                beats = jnp.where(rowk >= blocks[v], 1.0, 0.0)
            elif (v + 1) * sub <= k:
                beats = jnp.where(rowk > blocks[v], 1.0, 0.0)
            else:
                beats = jnp.where(jloc > k - v * sub,
                                  jnp.where(rowk >= blocks[v], 1.0, 0.0),
                                  jnp.where(rowk > blocks[v], 1.0, 0.0))
            ranks[v] = ranks[v] + beats
    rank = jnp.concatenate(ranks, axis=0)
    sel_bias_t = jnp.where(rank < float(SEL_TOPK), jnp.where(jj <= i, 0.0, NEG), NEG).astype(BF16)
    zeros_t = jnp.zeros_like(sel_bias_t)

    lane_rows = lax.broadcasted_iota(jnp.int32, (SEL_BLOCK, LANES), 1)
    tok_rows = lax.broadcasted_iota(jnp.int32, (SEL_BLOCK, LANES), 0)
    win_col = lax.broadcasted_iota(jnp.int32, (NSA_ROWS, WIN_KEYS), 1)
    gates = gate_ref[0]
    nfull = i // (SEL_CHUNK // SEL_BLOCK)
    outs = []
    for g in range(NSA_KV_HEADS):
        qr = stack_heads(qr_ref, g)
        pick = jnp.where(lane_rows == tok_rows + g * SEL_BLOCK, 1.0, 0.0).astype(BF16)
        xg = jnp.concatenate([zeros_t, sel_bias_t] if g == 0 else [sel_bias_t, zeros_t], axis=0)
        bias_w = _nt(pick, xg).astype(BF16)
        q_aug = qr + jnp.concatenate([bias_w] * NSA_GROUP, axis=0)

        m_ref[...] = jnp.full(m_ref.shape, NEG, F32)
        l_ref[...] = jnp.zeros(l_ref.shape, F32)
        acc_ref[...] = jnp.zeros(acc_ref.shape, F32)

        def sel_step(c, diag_bias, g=g, q_aug=q_aug):
            rows = pl.ds(pl.multiple_of(c * SEL_CHUNK, SEL_CHUNK), SEL_CHUNK)
            s = _nt(q_aug, kaug_ref[0, g, rows, :])
            if diag_bias is not None:
                s = s + diag_bias
            m_old = m_ref[...]
            m_new = jnp.maximum(m_old, jnp.max(s, axis=1, keepdims=True))
            alpha = jnp.exp(m_old - m_new)
            p = jnp.exp(s - m_new)
            l_ref[...] = alpha * l_ref[...] + jnp.sum(p, axis=1, keepdims=True)
            acc_ref[...] = alpha * acc_ref[...] + jnp.dot(p.astype(BF16), vs_ref[0, rows, :],
                                                          preferred_element_type=F32)
            m_ref[...] = m_new

        def sel_body(c, carry):
            sel_step(c, None)
            return carry

        lax.fori_loop(0, nfull, sel_body, 0)
        sel_step(nfull, tri_ref[i % (SEL_CHUNK // SEL_BLOCK)])
        o_sel = acc_ref[...] / l_ref[...]

        wrows = pl.ds(pl.multiple_of(t0, SEL_BLOCK), WIN_KEYS)
        s = _nt(qr, kw_ref[0, wrows, :]) + wbias_ref[...]
        s = jnp.where(win_col >= WINDOW - t0, s, NEG)
        mx = jnp.max(s, axis=1, keepdims=True)
        e = jnp.exp(s - mx)
        o_win = jnp.dot(e.astype(BF16), vw_ref[0, wrows, :], preferred_element_type=F32)
        o_win = o_win / jnp.sum(e, axis=1, keepdims=True)

        o_cmp = ocmp_ref[g]
        for r in range(NSA_GROUP):
            rs = slice(r * SEL_BLOCK, (r + 1) * SEL_BLOCK)
            c0 = (g * NSA_GROUP + r) * 3
            outs.append(gates[:, c0:c0 + 1] * o_cmp[rs] + gates[:, c0 + 1:c0 + 2] * o_sel[rs]
                        + gates[:, c0 + 2:c0 + 3] * o_win[rs])

    lower = lane_rows < HEAD_DIM
    for k in range(NSA_HEADS // 2):
        even, odd = outs[2 * k], outs[2 * k + 1]
        if k < NSA_HEADS // 4:
            slab = jnp.where(lower, even, pltpu.roll(odd, HEAD_DIM, 1))
        else:
            slab = jnp.where(lower, pltpu.roll(even, HEAD_DIM, 1), odd)
        o_ref[0, :, k * LANES:(k + 1) * LANES] = slab.astype(BF16)


def _nsa(proj3, gates3, kaug, kwp, vwp, kvcmp, ovt, wbias, tri):
    bsz, seq, _ = proj3.shape
    nblk = seq // SEL_BLOCK
    qw = NSA_HEADS * LANES
    ncmp = kvcmp.shape[2]
    return pl.pallas_call(
        _nsa_kernel,
        grid=(bsz, nblk),
        in_specs=[pl.BlockSpec((1, SEL_BLOCK, qw), lambda b, i: (b, i, COL_QR // qw)),
                  pl.BlockSpec((1, SEL_BLOCK, qw), lambda b, i: (b, i, COL_QC // qw)),
                  pl.BlockSpec((1, NSA_KV_HEADS, seq, LANES), lambda b, i: (b, 0, 0, 0)),
                  pl.BlockSpec((1, seq, LANES), lambda b, i: (b, 0, COL_VS // LANES)),
                  pl.BlockSpec((1, seq + WINDOW, LANES), lambda b, i: (b, 0, 0)),
                  pl.BlockSpec((1, seq + WINDOW, LANES), lambda b, i: (b, 0, 0)),
                  pl.BlockSpec((1, 1, ncmp, LANES), lambda b, i: (0, b, 0, 0)),
                  pl.BlockSpec((1, 1, ncmp, LANES), lambda b, i: (1, b, 0, 0)),
                  pl.BlockSpec((1, SEL_BLOCK, LANES), lambda b, i: (b, i, 0)),
                  pl.BlockSpec(ovt.shape, lambda b, i: (0, 0)),
                  pl.BlockSpec(wbias.shape, lambda b, i: (0, 0)),
                  pl.BlockSpec(tri.shape, lambda b, i: (0, 0, 0))],
        out_specs=pl.BlockSpec((1, SEL_BLOCK, NSA_HEADS * HEAD_DIM), lambda b, i: (b, i, 0)),
        out_shape=jax.ShapeDtypeStruct((bsz, seq, NSA_HEADS * HEAD_DIM), BF16),
        scratch_shapes=[pltpu.VMEM((NSA_ROWS, 1), F32), pltpu.VMEM((NSA_ROWS, 1), F32),
                        pltpu.VMEM((NSA_ROWS, LANES), F32),
                        pltpu.VMEM((NSA_KV_HEADS, NSA_ROWS, LANES), F32)],
        compiler_params=pltpu.CompilerParams(dimension_semantics=("arbitrary", "arbitrary"),
                                             vmem_limit_bytes=VMEM_LIMIT),
        name="nsa",
    )(proj3, proj3, kaug, proj3, kwp, vwp, kvcmp, kvcmp, gates3, ovt, wbias, tri)


DIFF_TQ = 256


def _diff_kernel(q_ref, k_ref, v_ref, lam_ref, ng_ref, o_ref, m1_ref, l1_ref, a1_ref, m2_ref, l2_ref, a2_ref,
                 *, lambda_init):
    i = pl.program_id(2)
    tq = DIFF_TQ
    q = q_ref[0]
    lane = lax.broadcasted_iota(jnp.int32, (tq, LANES), 1)
    zero = jnp.zeros_like(q)
    qs = (jnp.where(lane < HEAD_DIM, q, zero), jnp.where(lane >= HEAD_DIM, q, zero))
    stats = ((m1_ref, l1_ref, a1_ref), (m2_ref, l2_ref, a2_ref))
    for m_ref, l_ref, a_ref in stats:
        m_ref[...] = jnp.full(m_ref.shape, NEG, F32)
        l_ref[...] = jnp.zeros(l_ref.shape, F32)
        a_ref[...] = jnp.zeros(a_ref.shape, F32)

    def step(c, causal):
        rows = pl.ds(pl.multiple_of(c * tq, tq), tq)
        k = k_ref[0, rows, :]
        v = v_ref[0, rows, :]
        for qm, (m_ref, l_ref, a_ref) in zip(qs, stats):
            s = _nt(qm, k)
            if causal:
                kpos = lax.broadcasted_iota(jnp.int32, (tq, tq), 1)
                qpos = lax.broadcasted_iota(jnp.int32, (tq, tq), 0)
                s = jnp.where(kpos <= qpos, s, NEG)
            m_old = m_ref[...]
            m_new = jnp.maximum(m_old, jnp.max(s, axis=1, keepdims=True))
            alpha = jnp.exp(m_old - m_new)
            p = jnp.exp(s - m_new)
            l_ref[...] = alpha * l_ref[...] + jnp.sum(p, axis=1, keepdims=True)
            a_ref[...] = alpha * a_ref[...] + jnp.dot(p.astype(BF16), v, preferred_element_type=F32)
            m_ref[...] = m_new

    def body(c, carry):
        step(c, False)
        return carry

    lax.fori_loop(0, i, body, 0)
    step(i, True)

    lv = lam_ref[...]
    lam = (jnp.exp(jnp.sum(lv[0:1] * lv[1:2], axis=1, keepdims=True))
           - jnp.exp(jnp.sum(lv[2:3] * lv[3:4], axis=1, keepdims=True)) + lambda_init)
    o = a1_ref[...] / l1_ref[...] - lam * (a2_ref[...] / l2_ref[...])
    o = o * lax.rsqrt(jnp.mean(o * o, axis=-1, keepdims=True) + EPS) * ng_ref[...] * (1.0 - lambda_init)
    o_ref[0] = o.astype(BF16)


def _diff(proj3, lam_vec, norm_g, lambda_init):
    bsz, seq, _ = proj3.shape
    tq = DIFF_TQ
    kern = functools.partial(_diff_kernel, lambda_init=lambda_init)
    stat = [pltpu.VMEM((tq, 1), F32), pltpu.VMEM((tq, 1), F32), pltpu.VMEM((tq, LANES), F32)]
    return pl.pallas_call(
        kern,
        grid=(bsz, DIFF_HEADS, seq // tq),
        in_specs=[pl.BlockSpec((1, tq, LANES), lambda b, h, i: (b, i, COL_QD // LANES + h)),
                  pl.BlockSpec((1, seq, LANES), lambda b, h, i: (b, 0, COL_KD // LANES + h)),
                  pl.BlockSpec((1, seq, LANES), lambda b, h, i: (b, 0, COL_VD // LANES + h)),
                  pl.BlockSpec(lam_vec.shape, lambda b, h, i: (0, 0)),
                  pl.BlockSpec((1, LANES), lambda b, h, i: (0, 0))],
        out_specs=pl.BlockSpec((1, tq, LANES), lambda b, h, i: (b, i, h)),
        out_shape=jax.ShapeDtypeStruct((bsz, seq, DIFF_HEADS * LANES), BF16),
        scratch_shapes=stat + stat,
        compiler_params=pltpu.CompilerParams(dimension_semantics=("arbitrary", "arbitrary", "arbitrary"),
                                             vmem_limit_bytes=VMEM_LIMIT),
        name="diff",
    )(proj3, proj3, proj3, lam_vec, norm_g.reshape(1, LANES))


def _merge_kernel(x_ref, ada_ref, on_ref, od_ref, wg_ref, wbn_ref, wbd_ref, wo_ref, g_ref, b_ref, o_ref, *, alpha):
    d = x_ref.shape[1]
    x = x_ref[...]
    sh = ada_ref[0, 0:1, :]
    sc = ada_ref[0, 1:2, :]
    gate = ada_ref[0, 2:3, :]
    u = (x * (1.0 + sc) + sh).astype(BF16)
    gm = jax.nn.sigmoid(jnp.dot(u, wg_ref[...], preferred_element_type=F32))
    y_nsa = jnp.dot(on_ref[...], wbn_ref[...], preferred_element_type=F32)
    y_diff = jnp.dot(od_ref[...], wbd_ref[...], preferred_element_type=F32)
    mixed = (gm[:, 0:d] * y_nsa + gm[:, d:2 * d] * y_diff).astype(BF16)
    mix = jnp.dot(mixed, wo_ref[...], preferred_element_type=F32)
    o_ref[...] = _layer_norm(alpha * x + (1.0 + gate) * mix, g_ref[...], b_ref[...])


def _merge(x2, ada3, o_nsa, o_diff, w_gm, w_bn, w_bd, w_o, ln_g, ln_b, seq, alpha):
    m, d = x2.shape
    tm = min(512, seq)
    per_seq = seq // tm
    full = lambda a: pl.BlockSpec(a.shape, lambda i: (0,) * a.ndim)
    kern = functools.partial(_merge_kernel, alpha=alpha)
    return pl.pallas_call(
        kern,
        grid=(m // tm,),
        in_specs=[pl.BlockSpec((tm, d), lambda i: (i, 0)),
                  pl.BlockSpec((1, 6, d), lambda i: (i // per_seq, 0, 0)),
                  pl.BlockSpec((tm, o_nsa.shape[1]), lambda i: (i, 0)),
                  pl.BlockSpec((tm, o_diff.shape[1]), lambda i: (i, 0)),
                  full(w_gm), full(w_bn), full(w_bd), full(w_o),
                  pl.BlockSpec((1, d), lambda i: (0, 0)),
                  pl.BlockSpec((1, d), lambda i: (0, 0))],
        out_specs=pl.BlockSpec((tm, d), lambda i: (i, 0)),
        out_shape=jax.ShapeDtypeStruct((m, d), F32),
        compiler_params=pltpu.CompilerParams(dimension_semantics=("arbitrary",),
                                             vmem_limit_bytes=VMEM_LIMIT),
        name="merge",
    )(x2, ada3, o_nsa, o_diff, w_gm, w_bn, w_bd, w_o, ln_g.reshape(1, d), ln_b.reshape(1, d))


def _ffn_kernel(x_ref, ada_ref, wa_ref, wb_ref, wo_ref, g_ref, b_ref, o_ref, u_ref, acc_ref, *, alpha):
    f = pl.program_id(1)

    @pl.when(f == 0)
    def _():
        sh = ada_ref[0, 3:4, :]
        sc = ada_ref[0, 4:5, :]
        u_ref[...] = (x_ref[...] * (1.0 + sc) + sh).astype(BF16)
        acc_ref[...] = jnp.zeros(acc_ref.shape, F32)

    u = u_ref[...]
    a = jnp.dot(u, wa_ref[...], preferred_element_type=F32)
    b = jnp.dot(u, wb_ref[...], preferred_element_type=F32)
    h = ((a * jax.nn.sigmoid(a)) * b).astype(BF16)
    acc_ref[...] += jnp.dot(h, wo_ref[...], preferred_element_type=F32)

    @pl.when(f == pl.num_programs(1) - 1)
    def _():
        gate = ada_ref[0, 5:6, :]
        o_ref[...] = _layer_norm(alpha * x_ref[...] + (1.0 + gate) * acc_ref[...], g_ref[...], b_ref[...])


def _ffn(x2, ada3, w_in, w_out, ln_g, ln_b, seq, alpha):
    m, d = x2.shape
    dff = w_out.shape[0]
    tm = min(512, seq)
    tf = dff // 2
    nf = dff // tf
    per_seq = seq // tm
    kern = functools.partial(_ffn_kernel, alpha=alpha)
    return pl.pallas_call(
        kern,
        grid=(m // tm, nf),
        in_specs=[pl.BlockSpec((tm, d), lambda i, f: (i, 0)),
                  pl.BlockSpec((1, 6, d), lambda i, f: (i // per_seq, 0, 0)),
                  pl.BlockSpec((d, tf), lambda i, f: (0, f)),
                  pl.BlockSpec((d, tf), lambda i, f: (0, nf + f)),
                  pl.BlockSpec((tf, d), lambda i, f: (f, 0)),
                  pl.BlockSpec((1, d), lambda i, f: (0, 0)),
                  pl.BlockSpec((1, d), lambda i, f: (0, 0))],
        out_specs=pl.BlockSpec((tm, d), lambda i, f: (i, 0)),
        out_shape=jax.ShapeDtypeStruct((m, d), F32),
        scratch_shapes=[pltpu.VMEM((tm, d), BF16), pltpu.VMEM((tm, d), F32)],
        compiler_params=pltpu.CompilerParams(dimension_semantics=("arbitrary", "arbitrary"),
                                             vmem_limit_bytes=VMEM_LIMIT),
        name="ffn",
    )(x2, ada3, w_in, w_in, w_out, ln_g.reshape(1, d), ln_b.reshape(1, d))


def _rope_tables(seq):
    half = ROPE_DIMS // 2
    inv_freq = ROPE_THETA ** (-jnp.arange(half, dtype=F32) * 2.0 / ROPE_DIMS)
    ang = jnp.arange(seq, dtype=F32)[:, None] * inv_freq[None, :]
    cos, sin = jnp.cos(ang), jnp.sin(ang)
    ones = jnp.ones((seq, HEAD_DIM - ROPE_DIMS), F32)
    zeros = jnp.zeros((seq, HEAD_DIM - ROPE_DIMS), F32)
    z8 = jnp.zeros((seq, half), F32)
    cos_h = jnp.concatenate([cos, cos, ones], axis=1)
    sa_h = jnp.concatenate([-sin, z8, zeros], axis=1)
    sb_h = jnp.concatenate([z8, sin, zeros], axis=1)
    rep = lambda t: jnp.concatenate([t] * (LANES // HEAD_DIM), axis=1)
    return rep(cos_h), rep(sa_h), rep(sb_h)


def _mask_tables(seq):
    nb = seq // SEL_BLOCK
    nc = (seq - CMP_BLOCK) // CMP_STRIDE + 1
    ncp = seq // CMP_STRIDE
    cs = np.arange(nc) * CMP_STRIDE
    bs = np.arange(nb) * SEL_BLOCK
    ov = np.minimum(cs[:, None] + CMP_BLOCK, bs[None, :] + SEL_BLOCK) - np.maximum(cs[:, None], bs[None, :])
    ov = np.clip(ov, 0, None) / CMP_BLOCK
    ovt = np.zeros((SEL_BLOCK, ncp), np.float32)
    ovt[:nb, :nc] = ov.T
    tl = np.arange(NSA_ROWS) % SEL_BLOCK
    c = np.arange(WIN_KEYS)
    wbias = np.where((c[None, :] <= tl[:, None] + WINDOW) & (c[None, :] > tl[:, None]), 0.0, NEG).astype(np.float32)
    per = SEL_CHUNK // SEL_BLOCK
    kk = np.arange(SEL_CHUNK)
    tri = np.zeros((per, NSA_ROWS, SEL_CHUNK), np.float32)
    for d in range(per):
        tri[d] = np.where((kk[None, :] // SEL_BLOCK == d) & (kk[None, :] % SEL_BLOCK > tl[:, None]), NEG, 0.0)
    return jnp.asarray(ovt, BF16), jnp.asarray(wbias), jnp.asarray(tri)


def _proj_weight(w_in_l):
    d = w_in_l.shape[0]
    sizes = (512, 128, 128, 128, 128, 128, 128, 24, 512, 512, 512, 2 * d)
    offs = np.cumsum((0,) + sizes)
    q_n, kc, vc, ks, vs, kw, vw, g_n, q_d, k_d, v_d, g_m = [w_in_l[:, offs[k]:offs[k + 1]] for k in range(12)]
    qh = q_n.reshape(d, NSA_HEADS, HEAD_DIM)
    z = jnp.zeros_like(qh)
    first = jnp.arange(NSA_HEADS)[None, :, None] < NSA_GROUP
    q_wide = jnp.concatenate([jnp.where(first, qh, z), jnp.where(first, z, qh)], axis=-1).reshape(d, NSA_HEADS * LANES)
    pad = jnp.zeros((d, COL_QD - COL_GN - g_n.shape[1]), w_in_l.dtype)
    w = jnp.concatenate([q_wide, v_d, kc, vc, vs, vw, g_n, pad, q_d, ks, kw, q_wide, k_d], axis=1)
    assert w.shape[1] == PROJ_COLS
    scale = np.ones((1, PROJ_COLS), np.float32)
    scale[:, COL_QC:COL_QC + NSA_HEADS * LANES] = QK_SCALE
    scale[:, COL_QR:COL_QR + NSA_HEADS * LANES] = QK_SCALE
    scale[:, COL_QD:COL_QD + 512] = QK_SCALE
    return w.astype(BF16), jnp.asarray(scale), g_m.astype(BF16)


def kernel(x, c, w_ada, b_ada, w_in, cmp_pe_k, cmp_w1_k, cmp_w2_k, cmp_pe_v, cmp_w1_v, cmp_w2_v, diff_lambda, diff_norm_g, w_branch_nsa, w_branch_diff, w_out, ln1_g, ln1_b, w_ffn_in, w_ffn_out, ln2_g, ln2_b):
    bsz, seq, d = x.shape
    depth = w_ada.shape[0]
    assert seq % 1024 == 0 or seq in (256, 512), seq
    assert seq // SEL_BLOCK <= SEL_BLOCK
    alpha = (2 * depth) ** 0.25
    ncp = seq // CMP_STRIDE
    cos_t, sa_t, sb_t = _rope_tables(seq)
    ovt, wbias, tri = _mask_tables(seq)
    onehot = (jnp.arange(seq)[:, None] // SEL_BLOCK == jnp.arange(HEAD_DIM)[None, :]).astype(BF16)
    onehot = jnp.broadcast_to(onehot[None], (bsz, seq, HEAD_DIM))

    x2 = x.reshape(bsz * seq, d)
    for l in range(depth):
        lambda_init = 0.8 - 0.6 * math.exp(-0.3 * l)
        ada3 = _ada(c, w_ada[l], b_ada[l]).reshape(bsz, 6, d)
        w_proj, colscale, w_gm = _proj_weight(w_in[l])
        proj, gates = _proj(x2, ada3, w_proj, colscale, cos_t, sa_t, sb_t, seq)
        proj3 = proj.reshape(bsz, seq, PROJ_COLS)
        gates3 = gates.reshape(bsz, seq, LANES)

        def chunks(col):
            t = proj3[:, :, col:col + LANES].reshape(bsz, ncp, CMP_STRIDE, NSA_KV_HEADS, HEAD_DIM)
            return t.transpose(0, 3, 1, 2, 4).reshape(bsz, NSA_KV_HEADS, ncp, CMP_STRIDE * HEAD_DIM)

        xr = jnp.stack([chunks(COL_KC), chunks(COL_VC)])
        w1 = jnp.stack([cmp_w1_k[l], cmp_w1_v[l]]).astype(BF16)
        pe = jnp.stack([cmp_pe_k[l], cmp_pe_v[l]]).reshape(2, 1, CMP_BLOCK * HEAD_DIM)
        pe = jnp.broadcast_to(pe, (2, 8, CMP_BLOCK * HEAD_DIM)).astype(BF16)
        w2 = jnp.stack([cmp_w2_k[l], cmp_w2_v[l]]).astype(BF16)
        z2 = jnp.zeros_like(w2)
        w2w = jnp.stack([jnp.concatenate([w2, z2], axis=-1), jnp.concatenate([z2, w2], axis=-1)], axis=1)
        kvcmp = _compress(xr, w1, pe, w2w)

        ks = proj3[:, :, COL_KS:COL_KS + LANES]
        kaug = jnp.stack([jnp.concatenate([ks[:, :, :HEAD_DIM], onehot], axis=-1),
                          jnp.concatenate([onehot, ks[:, :, HEAD_DIM:]], axis=-1)], axis=1)
        front = ((0, 0), (WINDOW, 0), (0, 0))
        kwp = jnp.pad(proj3[:, :, COL_KW:COL_KW + LANES], front)
        vwp = jnp.pad(proj3[:, :, COL_VW:COL_VW + LANES], front)
        o_nsa = _nsa(proj3, gates3, kaug, kwp, vwp, kvcmp, ovt, wbias, tri)
        o_diff = _diff(proj3, diff_lambda[l], diff_norm_g[l], lambda_init)

        x2 = _merge(x2, ada3, o_nsa.reshape(bsz * seq, -1), o_diff.reshape(bsz * seq, -1), w_gm,
                    w_branch_nsa[l].astype(BF16), w_branch_diff[l].astype(BF16), w_out[l].astype(BF16),
                    ln1_g[l], ln1_b[l], seq, alpha)
        x2 = _ffn(x2, ada3, w_ffn_in[l].astype(BF16), w_ffn_out[l].astype(BF16), ln2_g[l], ln2_b[l], seq, alpha)
    return x2.reshape(bsz, seq, d)
```

```python
import functools
import math

import numpy as np
import jax
import jax.numpy as jnp
from jax import lax
from jax.experimental import pallas as pl
from jax.experimental.pallas import tpu as pltpu

F32 = jnp.float32
BF16 = jnp.bfloat16

HEAD_DIM = 64
NSA_HEADS = 8
NSA_KV_HEADS = 2
NSA_GROUP = NSA_HEADS // NSA_KV_HEADS
CMP_BLOCK = 32
CMP_STRIDE = 16
CMP_HIDDEN = 256
SEL_BLOCK = 64
SEL_TOPK = 16
WINDOW = 512
DIFF_HEADS = 4
ROPE_THETA = 500000.0
ROPE_DIMS = HEAD_DIM // 4
EPS = 1e-5
NEG = -1e30
FORCE_SCORE = 1e9
QK_SCALE = HEAD_DIM ** -0.5

LANES = 128
VMEM_LIMIT = 56 * 1024 * 1024

PROJ_TN = 768
COL_QC = 0
COL_VD = 1024
COL_KC = 1536
COL_VC = 1664
COL_VS = 1792
COL_VW = 1920
COL_GN = 2048
COL_QD = 2304
COL_KS = 2816
COL_KW = 2944
COL_QR = 3072
COL_KD = 4096
PROJ_COLS = 4608
ROPE_TILE0 = COL_QD // PROJ_TN
GATE_TILE = COL_GN // PROJ_TN


def _nt(a, b):
    return lax.dot_general(a, b, (((1,), (1,)), ((), ())), preferred_element_type=F32)


def _split_bf16(a):
    hi = a.astype(BF16)
    lo = (a - hi.astype(F32)).astype(BF16)
    return hi, lo


def _layer_norm(h, g, b):
    mu = jnp.mean(h, axis=-1, keepdims=True)
    d = h - mu
    var = jnp.mean(d * d, axis=-1, keepdims=True)
    return d * lax.rsqrt(var + EPS) * g + b


def _ada_kernel(c_ref, w_ref, b_ref, o_ref):
    c = c_ref[...]
    a = c * jax.nn.sigmoid(c)
    a_hi, a_lo = _split_bf16(a)
    w_hi, w_lo = _split_bf16(w_ref[...])
    acc = jnp.dot(a_hi, w_hi, preferred_element_type=F32)
    acc += jnp.dot(a_lo, w_hi, preferred_element_type=F32)
    acc += jnp.dot(a_hi, w_lo, preferred_element_type=F32)
    o_ref[...] = acc + b_ref[...]


def _ada(c, w, b):
    bsz, d = c.shape
    n = w.shape[1]
    tn = 1024
    return pl.pallas_call(
        _ada_kernel,
        grid=(n // tn,),
        in_specs=[pl.BlockSpec((bsz, d), lambda j: (0, 0)),
                  pl.BlockSpec((d, tn), lambda j: (0, j)),
                  pl.BlockSpec((1, tn), lambda j: (0, j))],
        out_specs=pl.BlockSpec((bsz, tn), lambda j: (0, j)),
        out_shape=jax.ShapeDtypeStruct((bsz, n), F32),
        compiler_params=pltpu.CompilerParams(dimension_semantics=("arbitrary",),
                                             vmem_limit_bytes=VMEM_LIMIT),
        name="ada",
    )(c, w, b.reshape(1, n))


def _proj_kernel(x_ref, ada_ref, w_ref, cs_ref, cos_ref, sa_ref, sb_ref, o_ref, g_ref, u_ref, acc_ref,
                 *, row_chunk):
    j = pl.program_id(1)
    tm, tn = acc_ref.shape

    @pl.when(j == 0)
    def _():
        sh = ada_ref[0, 0:1, :]
        sc = ada_ref[0, 1:2, :]
        u_ref[...] = (x_ref[...] * (1.0 + sc) + sh).astype(BF16)

    acc_ref[...] = jnp.dot(u_ref[...], w_ref[...], preferred_element_type=F32)

    @pl.when(j < ROPE_TILE0)
    def _():
        o_ref[...] = (acc_ref[...] * cs_ref[...]).astype(BF16)

    @pl.when(j == GATE_TILE)
    def _():
        off = COL_GN - GATE_TILE * PROJ_TN
        g_ref[...] = jax.nn.sigmoid(acc_ref[:, off:off + LANES])

    @pl.when(j >= ROPE_TILE0)
    def _():
        def body(rc, carry):
            rows = pl.ds(pl.multiple_of(rc * row_chunk, row_chunk), row_chunk)
            cosv = cos_ref[rows, :]
            sav = sa_ref[rows, :]
            sbv = sb_ref[rows, :]
            for s in range(tn // LANES):
                cols = slice(s * LANES, (s + 1) * LANES)
                a = acc_ref[rows, cols] * cs_ref[:, cols]
                r = a * cosv + pltpu.roll(a, LANES - ROPE_DIMS // 2, 1) * sav + pltpu.roll(a, ROPE_DIMS // 2, 1) * sbv
                o_ref[rows, cols] = r.astype(BF16)
            return carry

        lax.fori_loop(0, tm // row_chunk, body, 0)


def _proj(x2, ada3, w, colscale, cos_t, sa_t, sb_t, seq):
    m, d = x2.shape
    tm = min(1024, seq)
    tn = PROJ_TN
    per_seq = seq // tm
    kern = functools.partial(_proj_kernel, row_chunk=128)
    return pl.pallas_call(
        kern,
        grid=(m // tm, PROJ_COLS // tn),
        in_specs=[pl.BlockSpec((tm, d), lambda i, j: (i, 0)),
                  pl.BlockSpec((1, 6, d), lambda i, j: (i // per_seq, 0, 0)),
                  pl.BlockSpec((d, tn), lambda i, j: (0, j)),
                  pl.BlockSpec((1, tn), lambda i, j: (0, j)),
                  pl.BlockSpec((tm, LANES), lambda i, j: (i % per_seq, 0)),
                  pl.BlockSpec((tm, LANES), lambda i, j: (i % per_seq, 0)),
                  pl.BlockSpec((tm, LANES), lambda i, j: (i % per_seq, 0))],
        out_specs=[pl.BlockSpec((tm, tn), lambda i, j: (i, j)),
                   pl.BlockSpec((tm, LANES), lambda i, j: (i, 0))],
        out_shape=[jax.ShapeDtypeStruct((m, PROJ_COLS), BF16),
                   jax.ShapeDtypeStruct((m, LANES), F32)],
        scratch_shapes=[pltpu.VMEM((tm, d), BF16), pltpu.VMEM((tm, tn), F32)],
        compiler_params=pltpu.CompilerParams(dimension_semantics=("arbitrary", "arbitrary"),
                                             vmem_limit_bytes=VMEM_LIMIT),
        name="proj",
    )(x2, ada3, w, colscale, cos_t, sa_t, sb_t)


def _gelu_tanh(x):
    return x * (0.5 * (1.0 + jnp.tanh(math.sqrt(2.0 / math.pi) * (x + 0.044715 * (x * x * x)))))


def _compress_kernel(xr_ref, w1_ref, pe_ref, w2_ref, o_ref):
    half = w1_ref.shape[1] // 2
    w1_top = w1_ref[0, 0:half, :]
    w1_bot = w1_ref[0, half:2 * half, :]
    pe_row = jnp.dot(pe_ref[0], w1_ref[0], preferred_element_type=F32)[0:1, :]
    nrow = xr_ref.shape[3]
    out = jnp.zeros((nrow, LANES), F32)
    for g in range(NSA_KV_HEADS):
        x = xr_ref[0, 0, g]
        a = jnp.dot(x, w1_top, preferred_element_type=F32)
        b = jnp.dot(x, w1_bot, preferred_element_type=F32)
        h = a + pltpu.roll(b, nrow - 1, 0) + pe_row
        out = out + jnp.dot(_gelu_tanh(h).astype(BF16), w2_ref[0, g], preferred_element_type=F32)
    o_ref[0, 0] = out.astype(BF16)


def _compress(xr, w1, pe, w2w):
    _, bsz, ng, nrow, width = xr.shape
    return pl.pallas_call(
        _compress_kernel,
        grid=(2, bsz),
        in_specs=[pl.BlockSpec((1, 1, ng, nrow, width), lambda w, b: (w, b, 0, 0, 0)),
                  pl.BlockSpec((1, 2 * width, CMP_HIDDEN), lambda w, b: (w, 0, 0)),
                  pl.BlockSpec((1, 8, 2 * width), lambda w, b: (w, 0, 0)),
                  pl.BlockSpec((1, ng, CMP_HIDDEN, LANES), lambda w, b: (w, 0, 0, 0))],
        out_specs=pl.BlockSpec((1, 1, nrow, LANES), lambda w, b: (w, b, 0, 0)),
        out_shape=jax.ShapeDtypeStruct((2, bsz, nrow, LANES), BF16),
        compiler_params=pltpu.CompilerParams(dimension_semantics=("arbitrary", "arbitrary"),
                                             vmem_limit_bytes=VMEM_LIMIT),
        name="compress",
    )(xr, w1, pe, w2w)


NSA_ROWS = NSA_GROUP * SEL_BLOCK
NSA_LANES = NSA_KV_HEADS * NSA_ROWS
SEL_CHUNK = 512
WIN_ALIGN = 128
WIN_KEYS = WINDOW + WIN_ALIGN
ONES_ROWS = 16
VT_ROWS = HEAD_DIM + ONES_ROWS


def _nsa_kernel(qr_ref, qc_ref, ksel_ref, vst_ref, kw_ref, vwt_ref, kc_ref, vct_ref, gate_ref,
                ovt_ref, wbias_ref, o_ref, m_ref, acc_ref):
    i = pl.program_id(1)
    t0 = i * SEL_BLOCK
    ncmp = kc_ref.shape[2]

    def stack_heads(ref):
        return jnp.concatenate([ref[0, :, h * LANES:(h + 1) * LANES] for h in range(NSA_HEADS)], axis=0)

    def pv(vts, p):
        return jnp.concatenate([jnp.dot(vts[g], p[:, g * NSA_ROWS:(g + 1) * NSA_ROWS], preferred_element_type=F32)
                                for g in range(NSA_KV_HEADS)], axis=1)

    lane128 = lax.broadcasted_iota(jnp.int32, (SEL_BLOCK, LANES), 1)
    tok128 = lax.broadcasted_iota(jnp.int32, (SEL_BLOCK, LANES), 0)
    lower = lane128 < HEAD_DIM
    gates = gate_ref[0, 0]

    qc = stack_heads(qc_ref)
    col_t = t0 + (lax.broadcasted_iota(jnp.int32, (ncmp, NSA_LANES), 1) & (SEL_BLOCK - 1))
    cmp_end = lax.broadcasted_iota(jnp.int32, (ncmp, NSA_LANES), 0) * CMP_STRIDE + (CMP_BLOCK - 1)
    valid_c = cmp_end <= col_t
    any_valid = jnp.max(jnp.where(valid_c, 1.0, 0.0), axis=0, keepdims=True)
    s = jnp.where(valid_c, _nt(kc_ref[0, 0], qc), NEG)
    e = jnp.exp(s - jnp.max(s, axis=0, keepdims=True))
    e_hi, e_lo = _split_bf16(e)
    acc = pv([vct_ref[0, g] for g in range(NSA_KV_HEADS)], e_hi)
    inv = any_valid / acc[HEAD_DIM:HEAD_DIM + 1]
    out = (gates[0:1] * inv) * acc[0:HEAD_DIM]
    imp4 = (jnp.dot(ovt_ref[...], e_hi, preferred_element_type=F32)
            + jnp.dot(ovt_ref[...], e_lo, preferred_element_type=F32)) * inv
    imps = []
    for g in range(NSA_KV_HEADS):
        two = imp4[:, 2 * g * LANES:(2 * g + 1) * LANES] + imp4[:, (2 * g + 1) * LANES:(2 * g + 2) * LANES]
        imps.append(two + pltpu.roll(two, HEAD_DIM, 1))
    imp_t = jnp.where(lower, imps[0], imps[1])

    qr = stack_heads(qr_ref)
    win_blk = i // (WIN_ALIGN // SEL_BLOCK)
    win_par = i % (WIN_ALIGN // SEL_BLOCK)
    win_row = lax.broadcasted_iota(jnp.int32, (WIN_KEYS, NSA_LANES), 0)
    wrows = pl.ds(pl.multiple_of(win_blk * WIN_ALIGN, WIN_ALIGN), WIN_KEYS)
    wb = wbias_ref[win_par]
    s = _nt(kw_ref[0, wrows, :], qr) + jnp.concatenate([wb] * NSA_KV_HEADS, axis=1)
    s = jnp.where(win_row >= WINDOW + win_par * SEL_BLOCK - t0, s, NEG)
    e = jnp.exp(s - jnp.max(s, axis=0, keepdims=True)).astype(BF16)
    acc = pv([jnp.concatenate([vwt_ref[0, g, win_blk + n] for n in range(WIN_KEYS // WIN_ALIGN)], axis=1)
              for g in range(NSA_KV_HEADS)], e)
    out = out + (gates[2:3] * (1.0 / acc[HEAD_DIM:HEAD_DIM + 1])) * acc[0:HEAD_DIM]

    nb_pad = imp_t.shape[0]
    jj = tok128
    forced = (jj == 0) | (jj == i) | (jj == i - 1)
    val = jnp.where(jj > i, -jnp.inf, jnp.where(forced, FORCE_SCORE, imp_t))
    sub = 8
    blocks = [val[v * sub:(v + 1) * sub, :] for v in range(nb_pad // sub)]
    ranks = [jnp.zeros((sub, LANES), F32) for _ in blocks]
    jloc = lax.broadcasted_iota(jnp.int32, (sub, LANES), 0)
    for k in range(nb_pad):
        rowk = jnp.broadcast_to(val[k:k + 1, :], (sub, LANES))
        for v in range(nb_pad // sub):
            if v * sub > k:
                beats = jnp.where(rowk >= blocks[v], 1.0, 0.0)
            elif (v + 1) * sub <= k:
                beats = jnp.where(rowk > blocks[v], 1.0, 0.0)
            else:
                beats = jnp.where(jloc > k - v * sub,
                                  jnp.where(rowk >= blocks[v], 1.0, 0.0),
                                  jnp.where(rowk > blocks[v], 1.0, 0.0))
            ranks[v] = ranks[v] + beats
    rank = jnp.concatenate(ranks, axis=0)
    sel_bias_t = jnp.where(rank < float(SEL_TOPK), jnp.where(jj <= i, 0.0, NEG), NEG).astype(BF16)
    xg = jnp.concatenate([sel_bias_t, jnp.zeros_like(sel_bias_t)], axis=0)
    bias_rows = []
    for g in range(NSA_KV_HEADS):
        pick = jnp.where(lane128 == tok128 + g * SEL_BLOCK, 1.0, 0.0).astype(BF16)
        bias_rows += [_nt(pick, xg).astype(BF16)] * NSA_GROUP
    q_aug = jnp.concatenate([qr, jnp.concatenate(bias_rows, axis=0)], axis=1)

    m_ref[...] = jnp.full(m_ref.shape, NEG, F32)
    acc_ref[...] = jnp.zeros(acc_ref.shape, F32)
    per_chunk = SEL_CHUNK // SEL_BLOCK
    nfull = i // per_chunk

    def sel_step(c, diag):
        rows = pl.ds(pl.multiple_of(c * SEL_CHUNK, SEL_CHUNK), SEL_CHUNK)
        s = _nt(ksel_ref[0, rows, :], q_aug)
        if diag:
            off = lax.broadcasted_iota(jnp.int32, (SEL_CHUNK, NSA_LANES), 0) - (i % per_chunk) * SEL_BLOCK
            tl = lax.broadcasted_iota(jnp.int32, (SEL_CHUNK, NSA_LANES), 1) & (SEL_BLOCK - 1)
            s = jnp.where(off > tl, NEG, s)
        m_old = m_ref[...]
        m_new = jnp.maximum(m_old, jnp.max(s, axis=0, keepdims=True))
        p = jnp.exp(s - m_new).astype(BF16)
        acc_ref[...] = jnp.exp(m_old - m_new) * acc_ref[...] + pv([vst_ref[0, g, c] for g in range(NSA_KV_HEADS)], p)
        m_ref[...] = m_new

    def sel_body(c, carry):
        sel_step(c, False)
        return carry

    lax.fori_loop(0, nfull, sel_body, 0)
    sel_step(nfull, True)
    acc = acc_ref[...]
    out = out + (gates[1:2] * (1.0 / acc[HEAD_DIM:HEAD_DIM + 1])) * acc[0:HEAD_DIM]
    o_ref[0, 0] = out.astype(BF16)


def _nsa(proj3, gates_t, ksel, vst, kwp, vwt, kvcmp, vct, ovt, wbias):
    bsz, seq, _ = proj3.shape
    nblk = seq // SEL_BLOCK
    qw = NSA_HEADS * LANES
    ncmp = kvcmp.shape[2]
    whole = lambda a: pl.BlockSpec((1,) + a.shape[1:], lambda b, i: (b,) + (0,) * (a.ndim - 1))
    const = lambda a: pl.BlockSpec(a.shape, lambda b, i: (0,) * a.ndim)
    return pl.pallas_call(
        _nsa_kernel,
        grid=(bsz, nblk),
        in_specs=[pl.BlockSpec((1, SEL_BLOCK, qw), lambda b, i: (b, i, COL_QR // qw)),
                  pl.BlockSpec((1, SEL_BLOCK, qw), lambda b, i: (b, i, COL_QC // qw)),
                  whole(ksel), whole(vst), whole(kwp), whole(vwt),
                  pl.BlockSpec((1, 1, ncmp, LANES), lambda b, i: (0, b, 0, 0)),
                  whole(vct),
                  pl.BlockSpec((1, 1) + gates_t.shape[2:], lambda b, i: (b, i, 0, 0)),
                  const(ovt), const(wbias)],
        out_specs=pl.BlockSpec((1, 1, HEAD_DIM, NSA_LANES), lambda b, i: (b, i, 0, 0)),
        out_shape=jax.ShapeDtypeStruct((bsz, nblk, HEAD_DIM, NSA_LANES), BF16),
        scratch_shapes=[pltpu.VMEM((1, NSA_LANES), F32), pltpu.VMEM((VT_ROWS, NSA_LANES), F32)],
        compiler_params=pltpu.CompilerParams(dimension_semantics=("arbitrary", "arbitrary"),
                                             vmem_limit_bytes=VMEM_LIMIT),
        name="nsa",
    )(proj3, proj3, ksel, vst, kwp, vwt, kvcmp, vct, gates_t, ovt, wbias)


DIFF_TQ = 512
DIFF_VT_ROWS = 2 * HEAD_DIM + ONES_ROWS


def _diff_kernel(q_ref, k_ref, vt_ref, lam_ref, ng_ref, o_ref, m_ref, acc_ref, *, lambda_init):
    i = pl.program_id(2)
    tq = DIFF_TQ
    nv = 2 * HEAD_DIM
    q = q_ref[0]
    lane = lax.broadcasted_iota(jnp.int32, (tq, LANES), 1)
    zero = jnp.zeros_like(q)
    q_both = jnp.concatenate([jnp.where(lane < HEAD_DIM, q, zero), jnp.where(lane >= HEAD_DIM, q, zero)], axis=0)
    m_ref[...] = jnp.full(m_ref.shape, NEG, F32)
    acc_ref[...] = jnp.zeros(acc_ref.shape, F32)

    def step(c, causal):
        k = k_ref[0, pl.ds(pl.multiple_of(c * tq, tq), tq), :]
        s = _nt(k, q_both)
        if causal:
            kpos = lax.broadcasted_iota(jnp.int32, (tq, 2 * tq), 0)
            qpos = lax.broadcasted_iota(jnp.int32, (tq, 2 * tq), 1) & (tq - 1)
            s = jnp.where(kpos <= qpos, s, NEG)
        m_old = m_ref[...]
        m_new = jnp.maximum(m_old, jnp.max(s, axis=0, keepdims=True))
        p = jnp.exp(s - m_new).astype(BF16)
        acc_ref[...] = jnp.exp(m_old - m_new) * acc_ref[...] + jnp.dot(vt_ref[0, 0, c], p,
                                                                       preferred_element_type=F32)
        m_ref[...] = m_new

    def body(c, carry):
        step(c, False)
        return carry

    lax.fori_loop(0, i, body, 0)
    step(i, True)

    lv = lam_ref[...]
    lam = (jnp.exp(jnp.sum(lv[0:1] * lv[1:2], axis=1, keepdims=True))
           - jnp.exp(jnp.sum(lv[2:3] * lv[3:4], axis=1, keepdims=True)) + lambda_init)
    acc = acc_ref[...]
    on = acc[0:nv] * (1.0 / acc[nv:nv + 1])
    o = on[:, 0:tq] - lam * on[:, tq:2 * tq]
    o = o * lax.rsqrt(jnp.mean(o * o, axis=0, keepdims=True) + EPS) * ng_ref[...] * (1.0 - lambda_init)
    o_ref[0] = o.T.astype(BF16)


def _diff(proj3, vt, lam_vec, norm_g, lambda_init):
    bsz, seq, _ = proj3.shape
    tq = DIFF_TQ
    kern = functools.partial(_diff_kernel, lambda_init=lambda_init)
    ng = jnp.broadcast_to(norm_g.reshape(LANES, 1), (LANES, tq))
    return pl.pallas_call(
        kern,
        grid=(bsz, DIFF_HEADS, seq // tq),
        in_specs=[pl.BlockSpec((1, tq, LANES), lambda b, h, i: (b, i, COL_QD // LANES + h)),
                  pl.BlockSpec((1, seq, LANES), lambda b, h, i: (b, 0, COL_KD // LANES + h)),
                  pl.BlockSpec((1, 1) + vt.shape[2:], lambda b, h, i: (b, h, 0, 0, 0)),
                  pl.BlockSpec(lam_vec.shape, lambda b, h, i: (0, 0)),
                  pl.BlockSpec((LANES, tq), lambda b, h, i: (0, 0))],
        out_specs=pl.BlockSpec((1, tq, LANES), lambda b, h, i: (b, i, h)),
        out_shape=jax.ShapeDtypeStruct((bsz, seq, DIFF_HEADS * LANES), BF16),
        scratch_shapes=[pltpu.VMEM((1, 2 * tq), F32), pltpu.VMEM((DIFF_VT_ROWS, 2 * tq), F32)],
        compiler_params=pltpu.CompilerParams(dimension_semantics=("arbitrary", "arbitrary", "arbitrary"),
                                             vmem_limit_bytes=VMEM_LIMIT),
        name="diff",
    )(proj3, proj3, vt, lam_vec, ng)


def _merge_kernel(x_ref, ada_ref, on_ref, od_ref, wg_ref, wbn_ref, wbd_ref, wo_ref, g_ref, b_ref, o_ref, *, alpha):
    d = x_ref.shape[1]
    x = x_ref[...]
    sh = ada_ref[0, 0:1, :]
    sc = ada_ref[0, 1:2, :]
    gate = ada_ref[0, 2:3, :]
    u = (x * (1.0 + sc) + sh).astype(BF16)
    gm = jax.nn.sigmoid(jnp.dot(u, wg_ref[...], preferred_element_type=F32))
    y_nsa = jnp.dot(on_ref[...], wbn_ref[...], preferred_element_type=F32)
    y_diff = jnp.dot(od_ref[...], wbd_ref[...], preferred_element_type=F32)
    mixed = (gm[:, 0:d] * y_nsa + gm[:, d:2 * d] * y_diff).astype(BF16)
    mix = jnp.dot(mixed, wo_ref[...], preferred_element_type=F32)
    o_ref[...] = _layer_norm(alpha * x + (1.0 + gate) * mix, g_ref[...], b_ref[...])


def _merge(x2, ada3, o_nsa, o_diff, w_gm, w_bn, w_bd, w_o, ln_g, ln_b, seq, alpha):
    m, d = x2.shape
    tm = min(512, seq)
    per_seq = seq // tm
    full = lambda a: pl.BlockSpec(a.shape, lambda i: (0,) * a.ndim)
    kern = functools.partial(_merge_kernel, alpha=alpha)
    return pl.pallas_call(
        kern,
        grid=(m // tm,),
        in_specs=[pl.BlockSpec((tm, d), lambda i: (i, 0)),
                  pl.BlockSpec((1, 6, d), lambda i: (i // per_seq, 0, 0)),
                  pl.BlockSpec((tm, o_nsa.shape[1]), lambda i: (i, 0)),
                  pl.BlockSpec((tm, o_diff.shape[1]), lambda i: (i, 0)),
                  full(w_gm), full(w_bn), full(w_bd), full(w_o),
                  pl.BlockSpec((1, d), lambda i: (0, 0)),
                  pl.BlockSpec((1, d), lambda i: (0, 0))],
        out_specs=pl.BlockSpec((tm, d), lambda i: (i, 0)),
        out_shape=jax.ShapeDtypeStruct((m, d), F32),
        compiler_params=pltpu.CompilerParams(dimension_semantics=("arbitrary",),
                                             vmem_limit_bytes=VMEM_LIMIT),
        name="merge",
    )(x2, ada3, o_nsa, o_diff, w_gm, w_bn, w_bd, w_o, ln_g.reshape(1, d), ln_b.reshape(1, d))


def _ffn_kernel(x_ref, ada_ref, wa_ref, wb_ref, wo_ref, g_ref, b_ref, o_ref, u_ref, acc_ref, *, alpha):
    f = pl.program_id(1)

    @pl.when(f == 0)
    def _():
        sh = ada_ref[0, 3:4, :]
        sc = ada_ref[0, 4:5, :]
        u_ref[...] = (x_ref[...] * (1.0 + sc) + sh).astype(BF16)
        acc_ref[...] = jnp.zeros(acc_ref.shape, F32)

    u = u_ref[...]
    a = jnp.dot(u, wa_ref[...], preferred_element_type=F32)
    b = jnp.dot(u, wb_ref[...], preferred_element_type=F32)
    h = ((a * jax.nn.sigmoid(a)) * b).astype(BF16)
    acc_ref[...] += jnp.dot(h, wo_ref[...], preferred_element_type=F32)

    @pl.when(f == pl.num_programs(1) - 1)
    def _():
        gate = ada_ref[0, 5:6, :]
        o_ref[...] = _layer_norm(alpha * x_ref[...] + (1.0 + gate) * acc_ref[...], g_ref[...], b_ref[...])


def _ffn(x2, ada3, w_in, w_out, ln_g, ln_b, seq, alpha):
    m, d = x2.shape
    dff = w_out.shape[0]
    tm = min(512, seq)
    tf = dff // 2
    nf = dff // tf
    per_seq = seq // tm
    kern = functools.partial(_ffn_kernel, alpha=alpha)
    return pl.pallas_call(
        kern,
        grid=(m // tm, nf),
        in_specs=[pl.BlockSpec((tm, d), lambda i, f: (i, 0)),
                  pl.BlockSpec((1, 6, d), lambda i, f: (i // per_seq, 0, 0)),
                  pl.BlockSpec((d, tf), lambda i, f: (0, f)),
                  pl.BlockSpec((d, tf), lambda i, f: (0, nf + f)),
                  pl.BlockSpec((tf, d), lambda i, f: (f, 0)),
                  pl.BlockSpec((1, d), lambda i, f: (0, 0)),
                  pl.BlockSpec((1, d), lambda i, f: (0, 0))],
        out_specs=pl.BlockSpec((tm, d), lambda i, f: (i, 0)),
        out_shape=jax.ShapeDtypeStruct((m, d), F32),
        scratch_shapes=[pltpu.VMEM((tm, d), BF16), pltpu.VMEM((tm, d), F32)],
        compiler_params=pltpu.CompilerParams(dimension_semantics=("arbitrary", "arbitrary"),
                                             vmem_limit_bytes=VMEM_LIMIT),
        name="ffn",
    )(x2, ada3, w_in, w_in, w_out, ln_g.reshape(1, d), ln_b.reshape(1, d))


def _rope_tables(seq):
    half = ROPE_DIMS // 2
    inv_freq = ROPE_THETA ** (-jnp.arange(half, dtype=F32) * 2.0 / ROPE_DIMS)
    ang = jnp.arange(seq, dtype=F32)[:, None] * inv_freq[None, :]
    cos, sin = jnp.cos(ang), jnp.sin(ang)
    ones = jnp.ones((seq, HEAD_DIM - ROPE_DIMS), F32)
    zeros = jnp.zeros((seq, HEAD_DIM - ROPE_DIMS), F32)
    z8 = jnp.zeros((seq, half), F32)
    cos_h = jnp.concatenate([cos, cos, ones], axis=1)
    sa_h = jnp.concatenate([-sin, z8, zeros], axis=1)
    sb_h = jnp.concatenate([z8, sin, zeros], axis=1)
    rep = lambda t: jnp.concatenate([t] * (LANES // HEAD_DIM), axis=1)
    return rep(cos_h), rep(sa_h), rep(sb_h)


def _mask_tables(seq):
    nb = seq // SEL_BLOCK
    nc = (seq - CMP_BLOCK) // CMP_STRIDE + 1
    ncp = seq // CMP_STRIDE
    cs = np.arange(nc) * CMP_STRIDE
    bs = np.arange(nb) * SEL_BLOCK
    ov = np.minimum(cs[:, None] + CMP_BLOCK, bs[None, :] + SEL_BLOCK) - np.maximum(cs[:, None], bs[None, :])
    ov = np.clip(ov, 0, None) / CMP_BLOCK
    ovt = np.zeros((SEL_BLOCK, ncp), np.float32)
    ovt[:nb, :nc] = ov.T
    tl = np.arange(NSA_ROWS) % SEL_BLOCK
    c = np.arange(WIN_KEYS)
    wbias = np.stack([np.where((c[:, None] <= tl[None, :] + WINDOW + par * SEL_BLOCK)
                               & (c[:, None] > tl[None, :] + par * SEL_BLOCK), 0.0, NEG)
                      for par in range(WIN_ALIGN // SEL_BLOCK)]).astype(np.float32)
    return jnp.asarray(ovt, BF16), jnp.asarray(wbias)


def _proj_weight(w_in_l):
    d = w_in_l.shape[0]
    sizes = (512, 128, 128, 128, 128, 128, 128, 24, 512, 512, 512, 2 * d)
    offs = np.cumsum((0,) + sizes)
    q_n, kc, vc, ks, vs, kw, vw, g_n, q_d, k_d, v_d, g_m = [w_in_l[:, offs[k]:offs[k + 1]] for k in range(12)]
    qh = q_n.reshape(d, NSA_HEADS, HEAD_DIM)
    z = jnp.zeros_like(qh)
    first = jnp.arange(NSA_HEADS)[None, :, None] < NSA_GROUP
    q_wide = jnp.concatenate([jnp.where(first, qh, z), jnp.where(first, z, qh)], axis=-1).reshape(d, NSA_HEADS * LANES)
    pad = jnp.zeros((d, COL_QD - COL_GN - g_n.shape[1]), w_in_l.dtype)
    w = jnp.concatenate([q_wide, v_d, kc, vc, vs, vw, g_n, pad, q_d, ks, kw, q_wide, k_d], axis=1)
    assert w.shape[1] == PROJ_COLS
    scale = np.ones((1, PROJ_COLS), np.float32)
    scale[:, COL_QC:COL_QC + NSA_HEADS * LANES] = QK_SCALE
    scale[:, COL_QR:COL_QR + NSA_HEADS * LANES] = QK_SCALE
    scale[:, COL_QD:COL_QD + 512] = QK_SCALE
    return w.astype(BF16), jnp.asarray(scale), g_m.astype(BF16)


def kernel(x, c, w_ada, b_ada, w_in, cmp_pe_k, cmp_w1_k, cmp_w2_k, cmp_pe_v, cmp_w1_v, cmp_w2_v, diff_lambda, diff_norm_g, w_branch_nsa, w_branch_diff, w_out, ln1_g, ln1_b, w_ffn_in, w_ffn_out, ln2_g, ln2_b):
    bsz, seq, d = x.shape
    depth = w_ada.shape[0]
    assert seq % 1024 == 0 or seq in (256, 512), seq
    assert seq // SEL_BLOCK <= SEL_BLOCK
    alpha = (2 * depth) ** 0.25
    ncp = seq // CMP_STRIDE
    nblk = seq // SEL_BLOCK
    cos_t, sa_t, sb_t = _rope_tables(seq)
    ovt, wbias = _mask_tables(seq)
    onehot = (jnp.arange(seq)[:, None] // SEL_BLOCK == jnp.arange(HEAD_DIM)[None, :]).astype(BF16)
    onehot = jnp.broadcast_to(jnp.pad(onehot, ((0, 0), (0, LANES - HEAD_DIM)))[None], (bsz, seq, LANES))

    x2 = x.reshape(bsz * seq, d)
    for l in range(depth):
        lambda_init = 0.8 - 0.6 * math.exp(-0.3 * l)
        ada3 = _ada(c, w_ada[l], b_ada[l]).reshape(bsz, 6, d)
        w_proj, colscale, w_gm = _proj_weight(w_in[l])
        proj, gates = _proj(x2, ada3, w_proj, colscale, cos_t, sa_t, sb_t, seq)
        proj3 = proj.reshape(bsz, seq, PROJ_COLS)
        gates3 = gates.reshape(bsz, seq, LANES)

        def chunks(col):
            t = proj3[:, :, col:col + LANES].reshape(bsz, ncp, CMP_STRIDE, NSA_KV_HEADS, HEAD_DIM)
            return t.transpose(0, 3, 1, 2, 4).reshape(bsz, NSA_KV_HEADS, ncp, CMP_STRIDE * HEAD_DIM)

        xr = jnp.stack([chunks(COL_KC), chunks(COL_VC)])
        w1 = jnp.stack([cmp_w1_k[l], cmp_w1_v[l]]).astype(BF16)
        pe = jnp.stack([cmp_pe_k[l], cmp_pe_v[l]]).reshape(2, 1, CMP_BLOCK * HEAD_DIM)
        pe = jnp.broadcast_to(pe, (2, 8, CMP_BLOCK * HEAD_DIM)).astype(BF16)
        w2 = jnp.stack([cmp_w2_k[l], cmp_w2_v[l]]).astype(BF16)
        z2 = jnp.zeros_like(w2)
        w2w = jnp.stack([jnp.concatenate([w2, z2], axis=-1), jnp.concatenate([z2, w2], axis=-1)], axis=1)
        kvcmp = _compress(xr, w1, pe, w2w)

        def with_ones(t):
            return jnp.concatenate([t, jnp.ones(t.shape[:-2] + (ONES_ROWS, t.shape[-1]), t.dtype)], axis=-2)

        def keys_to_lanes(t, chunk):
            t = t.reshape(bsz, t.shape[1] // chunk, chunk, NSA_KV_HEADS, HEAD_DIM)
            return with_ones(t.transpose(0, 3, 1, 4, 2))

        ksel = jnp.concatenate([proj3[:, :, COL_KS:COL_KS + LANES], onehot], axis=-1)
        vst = keys_to_lanes(proj3[:, :, COL_VS:COL_VS + LANES], SEL_CHUNK)
        front = ((0, 0), (WINDOW, 0), (0, 0))
        kwp = jnp.pad(proj3[:, :, COL_KW:COL_KW + LANES], front)
        vwt = keys_to_lanes(jnp.pad(proj3[:, :, COL_VW:COL_VW + LANES], front), WIN_ALIGN)
        vct = keys_to_lanes(kvcmp[1], ncp)[:, :, 0]
        gates_t = gates3[:, :, :NSA_HEADS * 3].reshape(bsz, nblk, SEL_BLOCK, NSA_HEADS, 3)
        gates_t = gates_t.transpose(0, 1, 4, 3, 2).reshape(bsz, nblk, 3, NSA_LANES)
        gates_t = jnp.pad(gates_t, ((0, 0), (0, 0), (0, 8 - 3), (0, 0)))
        o_nsa_t = _nsa(proj3, gates_t, ksel, vst, kwp, vwt, kvcmp, vct, ovt, wbias)
        o_nsa = o_nsa_t.reshape(bsz, nblk, HEAD_DIM, NSA_HEADS, SEL_BLOCK)
        o_nsa = o_nsa.transpose(0, 1, 4, 3, 2).reshape(bsz, seq, NSA_HEADS * HEAD_DIM)

        vd = proj3[:, :, COL_VD:COL_VD + DIFF_HEADS * LANES].reshape(bsz, seq // DIFF_TQ, DIFF_TQ, DIFF_HEADS, LANES)
        vdt = with_ones(vd.transpose(0, 3, 1, 4, 2))
        o_diff = _diff(proj3, vdt, diff_lambda[l], diff_norm_g[l], lambda_init)

        x2 = _merge(x2, ada3, o_nsa.reshape(bsz * seq, -1), o_diff.reshape(bsz * seq, -1), w_gm,
                    w_branch_nsa[l].astype(BF16), w_branch_diff[l].astype(BF16), w_out[l].astype(BF16),
                    ln1_g[l], ln1_b[l], seq, alpha)
        x2 = _ffn(x2, ada3, w_ffn_in[l].astype(BF16), w_ffn_out[l].astype(BF16), ln2_g[l], ln2_b[l], seq, alpha)
    return x2.reshape(bsz, seq, d)
```

```python
import functools
import math

import numpy as np
import jax
import jax.numpy as jnp
from jax import lax
from jax.experimental import pallas as pl
from jax.experimental.pallas import tpu as pltpu

F32 = jnp.float32
BF16 = jnp.bfloat16

HEAD_DIM = 64
NSA_HEADS = 8
NSA_KV_HEADS = 2
NSA_GROUP = NSA_HEADS // NSA_KV_HEADS
CMP_BLOCK = 32
CMP_STRIDE = 16
CMP_HIDDEN = 256
SEL_BLOCK = 64
SEL_TOPK = 16
WINDOW = 512
DIFF_HEADS = 4
ROPE_THETA = 500000.0
ROPE_DIMS = HEAD_DIM // 4
EPS = 1e-5
NEG = -1e30
FORCE_SCORE = 1e9
QK_SCALE = HEAD_DIM ** -0.5

LANES = 128
VMEM_LIMIT = 56 * 1024 * 1024

PROJ_TN = 512
COL_QR = 0
COL_QC = 1024
COL_QD = 2048
COL_KD = 2560
COL_KS = 3072
COL_KW = 3200
COL_KC = 3584
COL_VC = 3712
COL_GN = 3840
PROJ_COLS = 4096
ROPE_TILES = (0, 1, 4, 5, 6)
GATE_TILE = COL_GN // PROJ_TN
ROW_VD = 0
ROW_VS = 512
ROW_VW = 640
VT_ROWS_ALL = 768
VT_CHUNK = 512
VW_CHUNK = 128


def _nt(a, b):
    return lax.dot_general(a, b, (((1,), (1,)), ((), ())), preferred_element_type=F32)


def _split_bf16(a):
    hi = a.astype(BF16)
    lo = (a - hi.astype(F32)).astype(BF16)
    return hi, lo


def _layer_norm(h, g, b):
    mu = jnp.mean(h, axis=-1, keepdims=True)
    d = h - mu
    var = jnp.mean(d * d, axis=-1, keepdims=True)
    return d * lax.rsqrt(var + EPS) * g + b


def _ada_kernel(c_ref, w_ref, b_ref, o_ref):
    c = c_ref[...]
    a = c * jax.nn.sigmoid(c)
    a_hi, a_lo = _split_bf16(a)
    w_hi, w_lo = _split_bf16(w_ref[...])
    acc = jnp.dot(a_hi, w_hi, preferred_element_type=F32)
    acc += jnp.dot(a_lo, w_hi, preferred_element_type=F32)
    acc += jnp.dot(a_hi, w_lo, preferred_element_type=F32)
    o_ref[...] = acc + b_ref[...]


def _ada(c, w, b):
    bsz, d = c.shape
    n = w.shape[1]
    tn = 1024
    return pl.pallas_call(
        _ada_kernel,
        grid=(n // tn,),
        in_specs=[pl.BlockSpec((bsz, d), lambda j: (0, 0)),
                  pl.BlockSpec((d, tn), lambda j: (0, j)),
                  pl.BlockSpec((1, tn), lambda j: (0, j))],
        out_specs=pl.BlockSpec((bsz, tn), lambda j: (0, j)),
        out_shape=jax.ShapeDtypeStruct((bsz, n), F32),
        compiler_params=pltpu.CompilerParams(dimension_semantics=("arbitrary",),
                                             vmem_limit_bytes=VMEM_LIMIT),
        name="ada",
    )(c, w, b.reshape(1, n))


def _proj_kernel(x_ref, ada_ref, w_ref, wtv_ref, cs_ref, cos_ref, sa_ref, sb_ref, o_ref, g_ref, vt_ref, vw_ref,
                 u_ref, acc_ref, *, row_chunk):
    j = pl.program_id(1)
    tm, tn = acc_ref.shape

    @pl.when(j == 0)
    def _():
        sh = ada_ref[0, 0:1, :]
        sc = ada_ref[0, 1:2, :]
        u_ref[...] = (x_ref[...] * (1.0 + sc) + sh).astype(BF16)
        vt = _nt(wtv_ref[...], u_ref[...]).astype(BF16)
        for cc in range(tm // VT_CHUNK):
            vt_ref[cc] = vt[ROW_VD:ROW_VW, cc * VT_CHUNK:(cc + 1) * VT_CHUNK]
        for cc in range(tm // VW_CHUNK):
            vw_ref[cc] = vt[ROW_VW:VT_ROWS_ALL, cc * VW_CHUNK:(cc + 1) * VW_CHUNK]

    acc_ref[...] = jnp.dot(u_ref[...], w_ref[...], preferred_element_type=F32)
    is_rope = functools.reduce(jnp.logical_or, [j == t for t in ROPE_TILES])

    @pl.when(jnp.logical_not(is_rope))
    def _():
        o_ref[...] = (acc_ref[...] * cs_ref[...]).astype(BF16)

    @pl.when(j == GATE_TILE)
    def _():
        off = COL_GN - GATE_TILE * PROJ_TN
        g_ref[...] = jax.nn.sigmoid(acc_ref[:, off:off + LANES])

    @pl.when(is_rope)
    def _():
        def body(rc, carry):
            rows = pl.ds(pl.multiple_of(rc * row_chunk, row_chunk), row_chunk)
            cosv = cos_ref[rows, :]
            sav = sa_ref[rows, :]
            sbv = sb_ref[rows, :]
            for s in range(tn // LANES):
                cols = slice(s * LANES, (s + 1) * LANES)
                a = acc_ref[rows, cols] * cs_ref[:, cols]
                r = a * cosv + pltpu.roll(a, LANES - ROPE_DIMS // 2, 1) * sav + pltpu.roll(a, ROPE_DIMS // 2, 1) * sbv
                o_ref[rows, cols] = r.astype(BF16)
            return carry

        lax.fori_loop(0, tm // row_chunk, body, 0)


def _proj(x2, ada3, w, wtv, colscale, cos_t, sa_t, sb_t, seq):
    m, d = x2.shape
    tm = min(1024, seq)
    tn = PROJ_TN
    per_seq = seq // tm
    kern = functools.partial(_proj_kernel, row_chunk=128)
    return pl.pallas_call(
        kern,
        grid=(m // tm, PROJ_COLS // tn),
        in_specs=[pl.BlockSpec((tm, d), lambda i, j: (i, 0)),
                  pl.BlockSpec((1, 6, d), lambda i, j: (i // per_seq, 0, 0)),
                  pl.BlockSpec((d, tn), lambda i, j: (0, j)),
                  pl.BlockSpec(wtv.shape, lambda i, j: (0, 0)),
                  pl.BlockSpec((1, tn), lambda i, j: (0, j)),
                  pl.BlockSpec((tm, LANES), lambda i, j: (i % per_seq, 0)),
                  pl.BlockSpec((tm, LANES), lambda i, j: (i % per_seq, 0)),
                  pl.BlockSpec((tm, LANES), lambda i, j: (i % per_seq, 0))],
        out_specs=[pl.BlockSpec((tm, tn), lambda i, j: (i, j)),
                   pl.BlockSpec((tm, LANES), lambda i, j: (i, 0)),
                   pl.BlockSpec((tm // VT_CHUNK, ROW_VW, VT_CHUNK), lambda i, j: (i, 0, 0)),
                   pl.BlockSpec((tm // VW_CHUNK, VT_ROWS_ALL - ROW_VW, VW_CHUNK), lambda i, j: (i, 0, 0))],
        out_shape=[jax.ShapeDtypeStruct((m, PROJ_COLS), BF16),
                   jax.ShapeDtypeStruct((m, LANES), F32),
                   jax.ShapeDtypeStruct((m // VT_CHUNK, ROW_VW, VT_CHUNK), BF16),
                   jax.ShapeDtypeStruct((m // VW_CHUNK, VT_ROWS_ALL - ROW_VW, VW_CHUNK), BF16)],
        scratch_shapes=[pltpu.VMEM((tm, d), BF16), pltpu.VMEM((tm, tn), F32)],
        compiler_params=pltpu.CompilerParams(dimension_semantics=("arbitrary", "arbitrary"),
                                             vmem_limit_bytes=VMEM_LIMIT),
        name="proj",
    )(x2, ada3, w, wtv, colscale, cos_t, sa_t, sb_t)


def _gelu_tanh(x):
    return x * (0.5 * (1.0 + jnp.tanh(math.sqrt(2.0 / math.pi) * (x + 0.044715 * (x * x * x)))))


def _compress_kernel(xr_ref, w1_ref, pe_ref, w2k_ref, w2vt_ref, k_ref, vt_ref):
    nrow = xr_ref.shape[3]
    half = w1_ref.shape[1] // 2

    def hidden(which, g):
        w1 = w1_ref[which]
        pe_row = jnp.dot(pe_ref[which], w1, preferred_element_type=F32)[0:1, :]
        x = xr_ref[which, 0, g]
        a = jnp.dot(x, w1[0:half], preferred_element_type=F32)
        b = jnp.dot(x, w1[half:2 * half], preferred_element_type=F32)
        return _gelu_tanh(a + pltpu.roll(b, nrow - 1, 0) + pe_row).astype(BF16)

    k_out = jnp.zeros((nrow, LANES), F32)
    vt_out = jnp.zeros((LANES, nrow), F32)
    for g in range(NSA_KV_HEADS):
        k_out = k_out + jnp.dot(hidden(0, g), w2k_ref[g], preferred_element_type=F32)
        vt_out = vt_out + _nt(w2vt_ref[g], hidden(1, g))
    k_ref[0] = k_out.astype(BF16)
    vt_ref[0] = vt_out.astype(BF16)


def _compress(xr, w1, pe, w2k, w2vt):
    _, bsz, ng, nrow, width = xr.shape
    const = lambda a: pl.BlockSpec(a.shape, lambda b: (0,) * a.ndim)
    return pl.pallas_call(
        _compress_kernel,
        grid=(bsz,),
        in_specs=[pl.BlockSpec((2, 1, ng, nrow, width), lambda b: (0, b, 0, 0, 0)),
                  const(w1), const(pe), const(w2k), const(w2vt)],
        out_specs=[pl.BlockSpec((1, nrow, LANES), lambda b: (b, 0, 0)),
                   pl.BlockSpec((1, LANES, nrow), lambda b: (b, 0, 0))],
        out_shape=[jax.ShapeDtypeStruct((bsz, nrow, LANES), BF16),
                   jax.ShapeDtypeStruct((bsz, LANES, nrow), BF16)],
        compiler_params=pltpu.CompilerParams(dimension_semantics=("arbitrary",),
                                             vmem_limit_bytes=VMEM_LIMIT),
        name="compress",
    )(xr, w1, pe, w2k, w2vt)


NSA_ROWS = NSA_GROUP * SEL_BLOCK
NSA_LANES = NSA_KV_HEADS * NSA_ROWS
SEL_CHUNK = VT_CHUNK
WIN_KEYS = WINDOW + VW_CHUNK
ONES_ROWS = 16


def _with_ones(vt):
    return jnp.concatenate([vt, jnp.ones((ONES_ROWS, vt.shape[1]), vt.dtype)], axis=0)


def _nsa_kernel(qr_ref, qc_ref, ks_ref, oh_ref, vs_ref, kw_ref, vw_ref, kc_ref, vct_ref, gate_ref,
                ovt_ref, dtab_ref, ctab_ref, o_ref, m_ref, acc_ref, s_ref):
    i = pl.program_id(1)
    t0 = i * SEL_BLOCK

    def stack_heads(ref):
        return jnp.concatenate([ref[0, :, h * LANES:(h + 1) * LANES] for h in range(NSA_HEADS)], axis=0)

    def pv(vts, p):
        return jnp.concatenate(
            [jnp.dot(_with_ones(vts[g * HEAD_DIM:(g + 1) * HEAD_DIM]), p[:, g * NSA_ROWS:(g + 1) * NSA_ROWS],
                     preferred_element_type=F32) for g in range(NSA_KV_HEADS)], axis=1)

    lane128 = lax.broadcasted_iota(jnp.int32, (SEL_BLOCK, LANES), 1)
    tok128 = lax.broadcasted_iota(jnp.int32, (SEL_BLOCK, LANES), 0)
    lower = lane128 < HEAD_DIM
    gates = gate_ref[0, 0]

    qc = stack_heads(qc_ref)
    ctab = ctab_ref[...]
    any_valid = jnp.where(ctab[0:1] <= t0, 1.0, 0.0)
    s = jnp.where(ctab <= t0, _nt(kc_ref[0], qc), NEG)
    e = jnp.exp(s - jnp.max(s, axis=0, keepdims=True))
    e_hi, e_lo = _split_bf16(e)
    acc = pv(vct_ref[0], e_hi)
    inv = any_valid / acc[HEAD_DIM:HEAD_DIM + 1]
    out = (gates[0:1] * inv) * acc[0:HEAD_DIM]
    imp4 = (jnp.dot(ovt_ref[...], e_hi, preferred_element_type=F32)
            + jnp.dot(ovt_ref[...], e_lo, preferred_element_type=F32)) * inv
    imps = []
    for g in range(NSA_KV_HEADS):
        two = imp4[:, 2 * g * LANES:(2 * g + 1) * LANES] + imp4[:, (2 * g + 1) * LANES:(2 * g + 2) * LANES]
        imps.append(two + pltpu.roll(two, HEAD_DIM, 1))
    imp_t = jnp.where(lower, imps[0], imps[1])

    qr = stack_heads(qr_ref)
    nwin = WIN_KEYS // VW_CHUNK
    wb0 = jnp.maximum(i // (VW_CHUNK // SEL_BLOCK) - WINDOW // VW_CHUNK, 0)
    wrows = pl.ds(pl.multiple_of(wb0 * VW_CHUNK, VW_CHUNK), WIN_KEYS)
    dist = (t0 - wb0 * VW_CHUNK) - dtab_ref[...]
    s = jnp.where(lax.bitcast_convert_type(dist, jnp.uint32) < WINDOW, _nt(kw_ref[0, wrows, :], qr), NEG)
    e = jnp.exp(s - jnp.max(s, axis=0, keepdims=True)).astype(BF16)
    acc = pv(jnp.concatenate([vw_ref[wb0 + n] for n in range(nwin)], axis=1), e)
    out = out + (gates[2:3] * (1.0 / acc[HEAD_DIM:HEAD_DIM + 1])) * acc[0:HEAD_DIM]

    nb_pad = imp_t.shape[0]
    jj = tok128
    forced = (jj == 0) | (jj == i) | (jj == i - 1)
    val = jnp.where(jj > i, -jnp.inf, jnp.where(forced, FORCE_SCORE, imp_t))
    sub = 8
    blocks = [val[v * sub:(v + 1) * sub, :] for v in range(nb_pad // sub)]
    ranks = [jnp.zeros((sub, LANES), F32) for _ in blocks]
    jloc = lax.broadcasted_iota(jnp.int32, (sub, LANES), 0)
    for k in range(nb_pad):
        rowk = jnp.broadcast_to(val[k:k + 1, :], (sub, LANES))
        for v in range(nb_pad // sub):
            if v * sub > k:
                beats = jnp.where(rowk >= blocks[v], 1.0, 0.0)
            elif (v + 1) * sub <= k:
                beats = jnp.where(rowk > blocks[v], 1.0, 0.0)
            else:
                beats = jnp.where(jloc > k - v * sub,
                                  jnp.where(rowk >= blocks[v], 1.0, 0.0),
                                  jnp.where(rowk > blocks[v], 1.0, 0.0))
            ranks[v] = ranks[v] + beats
    rank = jnp.concatenate(ranks, axis=0)
    sel_bias_t = jnp.where(rank < float(SEL_TOPK), jnp.where(jj <= i, 0.0, NEG), NEG).astype(BF16)
    xg = jnp.concatenate([sel_bias_t, jnp.zeros_like(sel_bias_t)], axis=0)
    bias_rows = []
    for g in range(NSA_KV_HEADS):
        pick = jnp.where(lane128 == tok128 + g * SEL_BLOCK, 1.0, 0.0).astype(BF16)
        bias_rows += [_nt(pick, xg).astype(BF16)] * NSA_GROUP
    q_aug = jnp.concatenate([qr, jnp.concatenate(bias_rows, axis=0)], axis=1)

    m_ref[...] = jnp.full(m_ref.shape, NEG, F32)
    acc_ref[...] = jnp.zeros(acc_ref.shape, F32)
    per_chunk = SEL_CHUNK // SEL_BLOCK
    nfull = i // per_chunk

    def scores(c, slot):
        rows = pl.ds(pl.multiple_of(c * SEL_CHUNK, SEL_CHUNK), SEL_CHUNK)
        k_aug = jnp.concatenate([ks_ref[0, rows, :], oh_ref[rows, :]], axis=1)
        s_ref[slot] = _nt(k_aug, q_aug)

    def absorb(c, slot, diag):
        s = s_ref[slot]
        if diag:
            s = jnp.where(dtab_ref[0:SEL_CHUNK, :] > (i % per_chunk) * SEL_BLOCK, NEG, s)
        m_old = m_ref[...]
        m_new = jnp.maximum(m_old, jnp.max(s, axis=0, keepdims=True))
        p = jnp.exp(s - m_new).astype(BF16)
        acc_ref[...] = jnp.exp(m_old - m_new) * acc_ref[...] + pv(vs_ref[c], p)
        m_ref[...] = m_new

    scores(0, 0)

    def pair(jp, carry):
        c = 2 * jp
        scores(c + 1, 1)
        absorb(c, 0, False)
        scores(c + 2, 0)
        absorb(c + 1, 1, False)
        return carry

    lax.fori_loop(0, nfull // 2, pair, 0)

    @pl.when(nfull % 2 == 1)
    def _():
        scores(nfull, 1)
        absorb(nfull - 1, 0, False)
        absorb(nfull, 1, True)

    @pl.when(nfull % 2 == 0)
    def _():
        absorb(nfull, 0, True)

    acc = acc_ref[...]
    out = (out + (gates[1:2] * (1.0 / acc[HEAD_DIM:HEAD_DIM + 1])) * acc[0:HEAD_DIM]).astype(BF16)

    fold = jnp.where((lane128 & (SEL_BLOCK - 1)) == tok128, 1.0, 0.0).astype(BF16)
    zero = jnp.zeros((HEAD_DIM, LANES), BF16)
    for k in range(NSA_HEADS // 2):
        x = out[:, k * LANES:(k + 1) * LANES]
        y = jnp.concatenate([jnp.where(lower, x, zero), jnp.where(lower, zero, x)], axis=0)
        o_ref[0, :, k * LANES:(k + 1) * LANES] = _nt(fold, y).astype(BF16)


def _nsa(proj3, gates_t, onehot, vt512, vw128, kcmp, vct, ovt, dtab, ctab):
    bsz, seq, _ = proj3.shape
    nblk = seq // SEL_BLOCK
    qw = NSA_HEADS * LANES
    const = lambda a: pl.BlockSpec(a.shape, lambda b, i: (0,) * a.ndim)
    per_b = lambda a: pl.BlockSpec((1,) + a.shape[1:], lambda b, i: (b,) + (0,) * (a.ndim - 1))
    return pl.pallas_call(
        _nsa_kernel,
        grid=(bsz, nblk),
        in_specs=[pl.BlockSpec((1, SEL_BLOCK, qw), lambda b, i: (b, i, COL_QR // qw)),
                  pl.BlockSpec((1, SEL_BLOCK, qw), lambda b, i: (b, i, COL_QC // qw)),
                  pl.BlockSpec((1, seq, LANES), lambda b, i: (b, 0, COL_KS // LANES)),
                  const(onehot),
                  pl.BlockSpec((seq // VT_CHUNK, LANES, VT_CHUNK), lambda b, i: (b, ROW_VS // LANES, 0)),
                  pl.BlockSpec((1, seq, LANES), lambda b, i: (b, 0, COL_KW // LANES)),
                  pl.BlockSpec((seq // VW_CHUNK, LANES, VW_CHUNK), lambda b, i: (b, 0, 0)),
                  per_b(kcmp), per_b(vct),
                  pl.BlockSpec((1, 1) + gates_t.shape[2:], lambda b, i: (b, i, 0, 0)),
                  const(ovt), const(dtab), const(ctab)],
        out_specs=pl.BlockSpec((1, SEL_BLOCK, NSA_HEADS * HEAD_DIM), lambda b, i: (b, i, 0)),
        out_shape=jax.ShapeDtypeStruct((bsz, seq, NSA_HEADS * HEAD_DIM), BF16),
        scratch_shapes=[pltpu.VMEM((1, NSA_LANES), F32), pltpu.VMEM((HEAD_DIM + ONES_ROWS, NSA_LANES), F32),
                        pltpu.VMEM((2, SEL_CHUNK, NSA_LANES), F32)],
        compiler_params=pltpu.CompilerParams(dimension_semantics=("arbitrary", "arbitrary"),
                                             vmem_limit_bytes=VMEM_LIMIT),
        name="nsa",
    )(proj3, proj3, proj3, onehot, vt512, proj3, vw128, kcmp, vct, gates_t, ovt, dtab, ctab)


DIFF_TQ = VT_CHUNK


def _diff_kernel(q_ref, k_ref, vt_ref, lam_ref, ng_ref, o_ref, m_ref, acc_ref, s_ref, *, lambda_init):
    i = pl.program_id(2)
    tq = DIFF_TQ
    nv = 2 * HEAD_DIM
    q = q_ref[0]
    lane = lax.broadcasted_iota(jnp.int32, (tq, LANES), 1)
    zero = jnp.zeros_like(q)
    q_both = jnp.concatenate([jnp.where(lane < HEAD_DIM, q, zero), jnp.where(lane >= HEAD_DIM, q, zero)], axis=0)
    m_ref[...] = jnp.full(m_ref.shape, NEG, F32)
    acc_ref[...] = jnp.zeros(acc_ref.shape, F32)

    def scores(c, slot):
        k = k_ref[0, pl.ds(pl.multiple_of(c * tq, tq), tq), :]
        s_ref[slot] = _nt(k, q_both)

    def absorb(c, slot, causal):
        s = s_ref[slot]
        if causal:
            kpos = lax.broadcasted_iota(jnp.int32, (tq, 2 * tq), 0)
            qpos = lax.broadcasted_iota(jnp.int32, (tq, 2 * tq), 1) & (tq - 1)
            s = jnp.where(kpos <= qpos, s, NEG)
        m_old = m_ref[...]
        m_new = jnp.maximum(m_old, jnp.max(s, axis=0, keepdims=True))
        p = jnp.exp(s - m_new).astype(BF16)
        acc_ref[...] = jnp.exp(m_old - m_new) * acc_ref[...] + jnp.dot(_with_ones(vt_ref[c]), p,
                                                                       preferred_element_type=F32)
        m_ref[...] = m_new

    scores(0, 0)

    def pair(jp, carry):
        c = 2 * jp
        scores(c + 1, 1)
        absorb(c, 0, False)
        scores(c + 2, 0)
        absorb(c + 1, 1, False)
        return carry

    lax.fori_loop(0, i // 2, pair, 0)

    @pl.when(i % 2 == 1)
    def _():
        scores(i, 1)
        absorb(i - 1, 0, False)
        absorb(i, 1, True)

    @pl.when(i % 2 == 0)
    def _():
        absorb(i, 0, True)

    lv = lam_ref[...]
    lam = (jnp.exp(jnp.sum(lv[0:1] * lv[1:2], axis=1, keepdims=True))
           - jnp.exp(jnp.sum(lv[2:3] * lv[3:4], axis=1, keepdims=True)) + lambda_init)
    acc = acc_ref[...]
    on = acc[0:nv] * (1.0 / acc[nv:nv + 1])
    o = on[:, 0:tq] - lam * on[:, tq:2 * tq]
    o = o * lax.rsqrt(jnp.mean(o * o, axis=0, keepdims=True) + EPS) * ng_ref[...] * (1.0 - lambda_init)
    o_ref[0] = o.T.astype(BF16)


def _diff(proj3, vt512, lam_vec, norm_g, lambda_init):
    bsz, seq, _ = proj3.shape
    tq = DIFF_TQ
    kern = functools.partial(_diff_kernel, lambda_init=lambda_init)
    ng = jnp.broadcast_to(norm_g.reshape(LANES, 1), (LANES, tq))
    return pl.pallas_call(
        kern,
        grid=(bsz, DIFF_HEADS, seq // tq),
        in_specs=[pl.BlockSpec((1, tq, LANES), lambda b, h, i: (b, i, COL_QD // LANES + h)),
                  pl.BlockSpec((1, seq, LANES), lambda b, h, i: (b, 0, COL_KD // LANES + h)),
                  pl.BlockSpec((seq // VT_CHUNK, LANES, VT_CHUNK), lambda b, h, i: (b, ROW_VD // LANES + h, 0)),
                  pl.BlockSpec(lam_vec.shape, lambda b, h, i: (0, 0)),
                  pl.BlockSpec((LANES, tq), lambda b, h, i: (0, 0))],
        out_specs=pl.BlockSpec((1, tq, LANES), lambda b, h, i: (b, i, h)),
        out_shape=jax.ShapeDtypeStruct((bsz, seq, DIFF_HEADS * LANES), BF16),
        scratch_shapes=[pltpu.VMEM((1, 2 * tq), F32), pltpu.VMEM((2 * HEAD_DIM + ONES_ROWS, 2 * tq), F32),
                        pltpu.VMEM((2, tq, 2 * tq), F32)],
        compiler_params=pltpu.CompilerParams(dimension_semantics=("arbitrary", "arbitrary", "arbitrary"),
                                             vmem_limit_bytes=VMEM_LIMIT),
        name="diff",
    )(proj3, proj3, vt512, lam_vec, ng)


def _merge_kernel(x_ref, ada_ref, on_ref, od_ref, wg_ref, wbn_ref, wbd_ref, wo_ref, g_ref, b_ref, o_ref, *, alpha):
    d = x_ref.shape[1]
    x = x_ref[...]
    sh = ada_ref[0, 0:1, :]
    sc = ada_ref[0, 1:2, :]
    gate = ada_ref[0, 2:3, :]
    u = (x * (1.0 + sc) + sh).astype(BF16)
    gm = jax.nn.sigmoid(jnp.dot(u, wg_ref[...], preferred_element_type=F32))
    y_nsa = jnp.dot(on_ref[...], wbn_ref[...], preferred_element_type=F32)
    y_diff = jnp.dot(od_ref[...], wbd_ref[...], preferred_element_type=F32)
    mixed = (gm[:, 0:d] * y_nsa + gm[:, d:2 * d] * y_diff).astype(BF16)
    mix = jnp.dot(mixed, wo_ref[...], preferred_element_type=F32)
    o_ref[...] = _layer_norm(alpha * x + (1.0 + gate) * mix, g_ref[...], b_ref[...])


def _merge(x2, ada3, o_nsa, o_diff, w_gm, w_bn, w_bd, w_o, ln_g, ln_b, seq, alpha):
    m, d = x2.shape
    tm = min(512, seq)
    per_seq = seq // tm
    full = lambda a: pl.BlockSpec(a.shape, lambda i: (0,) * a.ndim)
    kern = functools.partial(_merge_kernel, alpha=alpha)
    return pl.pallas_call(
        kern,
        grid=(m // tm,),
        in_specs=[pl.BlockSpec((tm, d), lambda i: (i, 0)),
                  pl.BlockSpec((1, 6, d), lambda i: (i // per_seq, 0, 0)),
                  pl.BlockSpec((tm, o_nsa.shape[1]), lambda i: (i, 0)),
                  pl.BlockSpec((tm, o_diff.shape[1]), lambda i: (i, 0)),
                  full(w_gm), full(w_bn), full(w_bd), full(w_o),
                  pl.BlockSpec((1, d), lambda i: (0, 0)),
                  pl.BlockSpec((1, d), lambda i: (0, 0))],
        out_specs=pl.BlockSpec((tm, d), lambda i: (i, 0)),
        out_shape=jax.ShapeDtypeStruct((m, d), F32),
        compiler_params=pltpu.CompilerParams(dimension_semantics=("arbitrary",),
                                             vmem_limit_bytes=VMEM_LIMIT),
        name="merge",
    )(x2, ada3, o_nsa, o_diff, w_gm, w_bn, w_bd, w_o, ln_g.reshape(1, d), ln_b.reshape(1, d))


def _ffn_kernel(x_ref, ada_ref, wa_ref, wb_ref, wo_ref, g_ref, b_ref, o_ref, u_ref, acc_ref, *, alpha):
    f = pl.program_id(1)

    @pl.when(f == 0)
    def _():
        sh = ada_ref[0, 3:4, :]
        sc = ada_ref[0, 4:5, :]
        u_ref[...] = (x_ref[...] * (1.0 + sc) + sh).astype(BF16)
        acc_ref[...] = jnp.zeros(acc_ref.shape, F32)

    u = u_ref[...]
    a = jnp.dot(u, wa_ref[...], preferred_element_type=F32)
    b = jnp.dot(u, wb_ref[...], preferred_element_type=F32)
    h = ((a * jax.nn.sigmoid(a)) * b).astype(BF16)
    acc_ref[...] += jnp.dot(h, wo_ref[...], preferred_element_type=F32)

    @pl.when(f == pl.num_programs(1) - 1)
    def _():
        gate = ada_ref[0, 5:6, :]
        o_ref[...] = _layer_norm(alpha * x_ref[...] + (1.0 + gate) * acc_ref[...], g_ref[...], b_ref[...])


def _ffn(x2, ada3, w_in, w_out, ln_g, ln_b, seq, alpha):
    m, d = x2.shape
    dff = w_out.shape[0]
    tm = min(512, seq)
    tf = dff // 2
    nf = dff // tf
    per_seq = seq // tm
    kern = functools.partial(_ffn_kernel, alpha=alpha)
    return pl.pallas_call(
        kern,
        grid=(m // tm, nf),
        in_specs=[pl.BlockSpec((tm, d), lambda i, f: (i, 0)),
                  pl.BlockSpec((1, 6, d), lambda i, f: (i // per_seq, 0, 0)),
                  pl.BlockSpec((d, tf), lambda i, f: (0, f)),
                  pl.BlockSpec((d, tf), lambda i, f: (0, nf + f)),
                  pl.BlockSpec((tf, d), lambda i, f: (f, 0)),
                  pl.BlockSpec((1, d), lambda i, f: (0, 0)),
                  pl.BlockSpec((1, d), lambda i, f: (0, 0))],
        out_specs=pl.BlockSpec((tm, d), lambda i, f: (i, 0)),
        out_shape=jax.ShapeDtypeStruct((m, d), F32),
        scratch_shapes=[pltpu.VMEM((tm, d), BF16), pltpu.VMEM((tm, d), F32)],
        compiler_params=pltpu.CompilerParams(dimension_semantics=("arbitrary", "arbitrary"),
                                             vmem_limit_bytes=VMEM_LIMIT),
        name="ffn",
    )(x2, ada3, w_in, w_in, w_out, ln_g.reshape(1, d), ln_b.reshape(1, d))


def _rope_tables(seq):
    half = ROPE_DIMS // 2
    inv_freq = ROPE_THETA ** (-jnp.arange(half, dtype=F32) * 2.0 / ROPE_DIMS)
    ang = jnp.arange(seq, dtype=F32)[:, None] * inv_freq[None, :]
    cos, sin = jnp.cos(ang), jnp.sin(ang)
    ones = jnp.ones((seq, HEAD_DIM - ROPE_DIMS), F32)
    zeros = jnp.zeros((seq, HEAD_DIM - ROPE_DIMS), F32)
    z8 = jnp.zeros((seq, half), F32)
    cos_h = jnp.concatenate([cos, cos, ones], axis=1)
    sa_h = jnp.concatenate([-sin, z8, zeros], axis=1)
    sb_h = jnp.concatenate([z8, sin, zeros], axis=1)
    rep = lambda t: jnp.concatenate([t] * (LANES // HEAD_DIM), axis=1)
    return rep(cos_h), rep(sa_h), rep(sb_h)


def _mask_tables(seq):
    nb = seq // SEL_BLOCK
    nc = (seq - CMP_BLOCK) // CMP_STRIDE + 1
    ncp = seq // CMP_STRIDE
    cs = np.arange(nc) * CMP_STRIDE
    bs = np.arange(nb) * SEL_BLOCK
    ov = np.minimum(cs[:, None] + CMP_BLOCK, bs[None, :] + SEL_BLOCK) - np.maximum(cs[:, None], bs[None, :])
    ov = np.clip(ov, 0, None) / CMP_BLOCK
    ovt = np.zeros((SEL_BLOCK, ncp), np.float32)
    ovt[:nb, :nc] = ov.T
    tl = np.arange(NSA_LANES) % SEL_BLOCK
    dtab = (np.arange(WIN_KEYS)[:, None] - tl[None, :]).astype(np.int32)
    ctab = (np.arange(ncp)[:, None] * CMP_STRIDE + (CMP_BLOCK - 1) - tl[None, :]).astype(np.int32)
    return jnp.asarray(ovt, BF16), jnp.asarray(dtab), jnp.asarray(ctab)


def _proj_weight(w_in_l):
    d = w_in_l.shape[0]
    sizes = (512, 128, 128, 128, 128, 128, 128, 24, 512, 512, 512, 2 * d)
    offs = np.cumsum((0,) + sizes)
    q_n, kc, vc, ks, vs, kw, vw, g_n, q_d, k_d, v_d, g_m = [w_in_l[:, offs[k]:offs[k + 1]] for k in range(12)]
    qh = q_n.reshape(d, NSA_HEADS, HEAD_DIM)
    z = jnp.zeros_like(qh)
    first = jnp.arange(NSA_HEADS)[None, :, None] < NSA_GROUP
    q_wide = jnp.concatenate([jnp.where(first, qh, z), jnp.where(first, z, qh)], axis=-1).reshape(d, NSA_HEADS * LANES)
    zeros = lambda n: jnp.zeros((d, n), w_in_l.dtype)
    w = jnp.concatenate([q_wide, q_wide, q_d, k_d, ks, kw, zeros(COL_KC - COL_KW - LANES),
                         kc, vc, g_n, zeros(PROJ_COLS - COL_GN - g_n.shape[1])], axis=1)
    assert w.shape[1] == PROJ_COLS
    wtv = jnp.concatenate([v_d, vs, vw], axis=1).T
    scale = np.ones((1, PROJ_COLS), np.float32)
    scale[:, COL_QR:COL_QD + 512] = QK_SCALE
    return w.astype(BF16), wtv.astype(BF16), jnp.asarray(scale), g_m.astype(BF16)


def kernel(x, c, w_ada, b_ada, w_in, cmp_pe_k, cmp_w1_k, cmp_w2_k, cmp_pe_v, cmp_w1_v, cmp_w2_v, diff_lambda, diff_norm_g, w_branch_nsa, w_branch_diff, w_out, ln1_g, ln1_b, w_ffn_in, w_ffn_out, ln2_g, ln2_b):
    bsz, seq, d = x.shape
    depth = w_ada.shape[0]
    assert seq % 1024 == 0, seq
    assert seq // SEL_BLOCK <= SEL_BLOCK
    alpha = (2 * depth) ** 0.25
    ncp = seq // CMP_STRIDE
    nblk = seq // SEL_BLOCK
    cos_t, sa_t, sb_t = _rope_tables(seq)
    ovt, dtab, ctab = _mask_tables(seq)
    onehot = (jnp.arange(seq)[:, None] // SEL_BLOCK == jnp.arange(LANES)[None, :]).astype(BF16)

    x2 = x.reshape(bsz * seq, d)
    for l in range(depth):
        lambda_init = 0.8 - 0.6 * math.exp(-0.3 * l)
        ada3 = _ada(c, w_ada[l], b_ada[l]).reshape(bsz, 6, d)
        w_proj, wtv, colscale, w_gm = _proj_weight(w_in[l])
        proj, gates, vt512, vw128 = _proj(x2, ada3, w_proj, wtv, colscale, cos_t, sa_t, sb_t, seq)
        proj3 = proj.reshape(bsz, seq, PROJ_COLS)

        def chunks(col):
            t = proj3[:, :, col:col + LANES].reshape(bsz, ncp, CMP_STRIDE, NSA_KV_HEADS, HEAD_DIM)
            return t.transpose(0, 3, 1, 2, 4).reshape(bsz, NSA_KV_HEADS, ncp, CMP_STRIDE * HEAD_DIM)

        xr = jnp.stack([chunks(COL_KC), chunks(COL_VC)])
        w1 = jnp.stack([cmp_w1_k[l], cmp_w1_v[l]]).astype(BF16)
        pe = jnp.stack([cmp_pe_k[l], cmp_pe_v[l]]).reshape(2, 1, CMP_BLOCK * HEAD_DIM)
        pe = jnp.broadcast_to(pe, (2, 8, CMP_BLOCK * HEAD_DIM)).astype(BF16)
        w2k = cmp_w2_k[l].astype(BF16)
        w2v = cmp_w2_v[l].astype(BF16).T
        w2k = jnp.stack([jnp.concatenate([w2k, jnp.zeros_like(w2k)], axis=1),
                         jnp.concatenate([jnp.zeros_like(w2k), w2k], axis=1)])
        w2vt = jnp.stack([jnp.concatenate([w2v, jnp.zeros_like(w2v)], axis=0),
                          jnp.concatenate([jnp.zeros_like(w2v), w2v], axis=0)])
        kcmp, vct = _compress(xr, w1, pe, w2k, w2vt)

        gates_t = gates.reshape(bsz, nblk, SEL_BLOCK, LANES)[..., :NSA_HEADS * 3]
        gates_t = gates_t.reshape(bsz, nblk, SEL_BLOCK, NSA_HEADS, 3)
        gates_t = gates_t.transpose(0, 1, 4, 3, 2).reshape(bsz, nblk, 3, NSA_LANES)
        gates_t = jnp.pad(gates_t, ((0, 0), (0, 0), (0, 8 - 3), (0, 0)))
        o_nsa = _nsa(proj3, gates_t, onehot, vt512, vw128, kcmp, vct, ovt, dtab, ctab)
        o_diff = _diff(proj3, vt512, diff_lambda[l], diff_norm_g[l], lambda_init)

        x2 = _merge(x2, ada3, o_nsa.reshape(bsz * seq, -1), o_diff.reshape(bsz * seq, -1), w_gm,
                    w_branch_nsa[l].astype(BF16), w_branch_diff[l].astype(BF16), w_out[l].astype(BF16),
                    ln1_g[l], ln1_b[l], seq, alpha)
        x2 = _ffn(x2, ada3, w_ffn_in[l].astype(BF16), w_ffn_out[l].astype(BF16), ln2_g[l], ln2_b[l], seq, alpha)
    return x2.reshape(bsz, seq, d)
```

```python
import functools
import math

import numpy as np
import jax
import jax.numpy as jnp
from jax import lax
from jax.experimental import pallas as pl
from jax.experimental.pallas import tpu as pltpu

F32 = jnp.float32
BF16 = jnp.bfloat16

HEAD_DIM = 64
NSA_HEADS = 8
NSA_KV_HEADS = 2
NSA_GROUP = NSA_HEADS // NSA_KV_HEADS
CMP_BLOCK = 32
CMP_STRIDE = 16
CMP_HIDDEN = 256
SEL_BLOCK = 64
SEL_TOPK = 16
WINDOW = 512
DIFF_HEADS = 4
ROPE_THETA = 500000.0
ROPE_DIMS = HEAD_DIM // 4
EPS = 1e-5
NEG = -1e30
FORCE_SCORE = 1e9
QK_SCALE = HEAD_DIM ** -0.5

LANES = 128
VMEM_LIMIT = 56 * 1024 * 1024

PROJ_TN = 1024
MXU_COLS = 256
COL_QR = 0
COL_QC = 1024
COL_QD = 2048
COL_KD = 2560
COL_KS = 3072
COL_KW = 3200
COL_GN = 3328
COL_KV = 3584
PROJ_COLS = 4096
ROPE_TILES = (0, 2)
PLAIN_TILES = (1,)
LAST_TILE = 3
ROW_VD = 0
ROW_VS = 512
ROW_VW = 640
VT_ROWS_ALL = 768
VT_CHUNK = 512
VW_CHUNK = 128


def _nt(a, b):
    return lax.dot_general(a, b, (((1,), (1,)), ((), ())), preferred_element_type=F32)


def _split_bf16(a):
    hi = a.astype(BF16)
    lo = (a - hi.astype(F32)).astype(BF16)
    return hi, lo


def _layer_norm(h, g, b):
    mu = jnp.mean(h, axis=-1, keepdims=True)
    d = h - mu
    var = jnp.mean(d * d, axis=-1, keepdims=True)
    return d * lax.rsqrt(var + EPS) * g + b


def _ada_kernel(c_ref, w_ref, b_ref, o_ref):
    c = c_ref[...]
    a = c * jax.nn.sigmoid(c)
    a_hi, a_lo = _split_bf16(a)
    w_hi, w_lo = _split_bf16(w_ref[...])
    acc = jnp.dot(a_hi, w_hi, preferred_element_type=F32)
    acc += jnp.dot(a_lo, w_hi, preferred_element_type=F32)
    acc += jnp.dot(a_hi, w_lo, preferred_element_type=F32)
    o_ref[...] = acc + b_ref[...]


def _ada(c, w, b):
    bsz, d = c.shape
    n = w.shape[1]
    tn = 1024
    return pl.pallas_call(
        _ada_kernel,
        grid=(n // tn,),
        in_specs=[pl.BlockSpec((bsz, d), lambda j: (0, 0)),
                  pl.BlockSpec((d, tn), lambda j: (0, j)),
                  pl.BlockSpec((1, tn), lambda j: (0, j))],
        out_specs=pl.BlockSpec((bsz, tn), lambda j: (0, j)),
        out_shape=jax.ShapeDtypeStruct((bsz, n), F32),
        compiler_params=pltpu.CompilerParams(dimension_semantics=("arbitrary",),
                                             vmem_limit_bytes=VMEM_LIMIT),
        name="ada",
    )(c, w, b.reshape(1, n))


def _proj_kernel(x_ref, ada_ref, w_ref, wtv_ref, cs_ref, cos_ref, sa_ref, sb_ref,
                 o_ref, g_ref, vt_ref, vw_ref, kc0_ref, kc1_ref, vc0_ref, vc1_ref, u_ref):
    j = pl.program_id(1)
    tm = u_ref.shape[0]
    tn = w_ref.shape[1]

    @pl.when(j == 0)
    def _():
        sh = ada_ref[0, 0:1, :]
        sc = ada_ref[0, 1:2, :]
        u_ref[...] = (x_ref[...] * (1.0 + sc) + sh).astype(BF16)
        vt = _nt(wtv_ref[...], u_ref[...]).astype(BF16)
        for cc in range(tm // VT_CHUNK):
            vt_ref[cc] = vt[ROW_VD:ROW_VW, cc * VT_CHUNK:(cc + 1) * VT_CHUNK]
        for cc in range(tm // VW_CHUNK):
            vw_ref[cc] = vt[ROW_VW:VT_ROWS_ALL, cc * VW_CHUNK:(cc + 1) * VW_CHUNK]

    def col_tile(ct):
        cols = slice(ct * MXU_COLS, (ct + 1) * MXU_COLS)
        return jnp.dot(u_ref[...], w_ref[:, cols], preferred_element_type=F32) * cs_ref[:, cols]

    def store_rope(a, ct):
        cosv, sav, sbv = cos_ref[...], sa_ref[...], sb_ref[...]
        for s in range(MXU_COLS // LANES):
            a_s = a[:, s * LANES:(s + 1) * LANES]
            r = a_s * cosv + pltpu.roll(a_s, LANES - ROPE_DIMS // 2, 1) * sav + pltpu.roll(a_s, ROPE_DIMS // 2, 1) * sbv
            o_ref[:, ct * MXU_COLS + s * LANES:ct * MXU_COLS + (s + 1) * LANES] = r.astype(BF16)

    def store_plain(a, ct):
        o_ref[:, ct * MXU_COLS:(ct + 1) * MXU_COLS] = a.astype(BF16)

    def any_of(tiles):
        return functools.reduce(jnp.logical_or, [j == t for t in tiles])

    @pl.when(any_of(ROPE_TILES))
    def _():
        for ct in range(tn // MXU_COLS):
            store_rope(col_tile(ct), ct)

    @pl.when(any_of(PLAIN_TILES))
    def _():
        for ct in range(tn // MXU_COLS):
            store_plain(col_tile(ct), ct)

    @pl.when(j == LAST_TILE)
    def _():
        gate_ct = (COL_GN - LAST_TILE * PROJ_TN) // MXU_COLS
        kv_ct = (COL_KV - LAST_TILE * PROJ_TN) // MXU_COLS
        kv_refs = ((kc0_ref, kc1_ref), (vc0_ref, vc1_ref))
        for ct in range(tn // MXU_COLS):
            a = col_tile(ct)
            if ct < kv_ct:
                if ct == gate_ct:
                    g_ref[...] = jax.nn.sigmoid(a[:, 0:LANES])
                store_rope(a, ct)
            else:
                for g, ref in enumerate(kv_refs[ct - kv_ct]):
                    ref[...] = a[:, g * LANES:g * LANES + HEAD_DIM].astype(BF16)
                store_plain(a, ct)


def _proj(x2, ada3, w, wtv, colscale, cos_t, sa_t, sb_t, seq):
    m, d = x2.shape
    tm = min(1024, seq)
    tn = PROJ_TN
    per_seq = seq // tm
    kv_spec = pl.BlockSpec((tm, HEAD_DIM), lambda i, j: (i, 0))
    kv_shape = jax.ShapeDtypeStruct((m, HEAD_DIM), BF16)
    return pl.pallas_call(
        _proj_kernel,
        grid=(m // tm, w.shape[1] // tn),
        in_specs=[pl.BlockSpec((tm, d), lambda i, j: (i, 0)),
                  pl.BlockSpec((1, 6, d), lambda i, j: (i // per_seq, 0, 0)),
                  pl.BlockSpec((d, tn), lambda i, j: (0, j)),
                  pl.BlockSpec(wtv.shape, lambda i, j: (0, 0)),
                  pl.BlockSpec((1, tn), lambda i, j: (0, j)),
                  pl.BlockSpec((tm, LANES), lambda i, j: (i % per_seq, 0)),
                  pl.BlockSpec((tm, LANES), lambda i, j: (i % per_seq, 0)),
                  pl.BlockSpec((tm, LANES), lambda i, j: (i % per_seq, 0))],
        out_specs=[pl.BlockSpec((tm, tn), lambda i, j: (i, j)),
                   pl.BlockSpec((tm, LANES), lambda i, j: (i, 0)),
                   pl.BlockSpec((tm // VT_CHUNK, ROW_VW, VT_CHUNK), lambda i, j: (i, 0, 0)),
                   pl.BlockSpec((tm // VW_CHUNK, VT_ROWS_ALL - ROW_VW, VW_CHUNK), lambda i, j: (i, 0, 0)),
                   kv_spec, kv_spec, kv_spec, kv_spec],
        out_shape=[jax.ShapeDtypeStruct((m, PROJ_COLS), BF16),
                   jax.ShapeDtypeStruct((m, LANES), F32),
                   jax.ShapeDtypeStruct((m // VT_CHUNK, ROW_VW, VT_CHUNK), BF16),
                   jax.ShapeDtypeStruct((m // VW_CHUNK, VT_ROWS_ALL - ROW_VW, VW_CHUNK), BF16),
                   kv_shape, kv_shape, kv_shape, kv_shape],
        scratch_shapes=[pltpu.VMEM((tm, d), BF16)],
        compiler_params=pltpu.CompilerParams(dimension_semantics=("arbitrary", "arbitrary"),
                                             vmem_limit_bytes=VMEM_LIMIT),
        name="proj",
    )(x2, ada3, w, wtv, colscale, cos_t, sa_t, sb_t)


def _gelu_tanh(x):
    return x * (0.5 * (1.0 + jnp.tanh(math.sqrt(2.0 / math.pi) * (x + 0.044715 * (x * x * x)))))


def _compress_kernel(kc0_ref, kc1_ref, vc0_ref, vc1_ref, w1_ref, pe_ref, w2k_ref, w2vt_ref, k_ref, vt_ref):
    xs = ((kc0_ref, kc1_ref), (vc0_ref, vc1_ref))
    nrow = kc0_ref.shape[1]
    half = w1_ref.shape[1] // 2

    def hidden(which, g):
        w1 = w1_ref[which]
        pe_row = jnp.dot(pe_ref[which], w1, preferred_element_type=F32)[0:1, :]
        x = xs[which][g][0]
        a = jnp.dot(x, w1[0:half], preferred_element_type=F32)
        b = jnp.dot(x, w1[half:2 * half], preferred_element_type=F32)
        return _gelu_tanh(a + pltpu.roll(b, nrow - 1, 0) + pe_row).astype(BF16)

    k_out = jnp.zeros((nrow, LANES), F32)
    vt_out = jnp.zeros((LANES, nrow), F32)
    for g in range(NSA_KV_HEADS):
        k_out = k_out + jnp.dot(hidden(0, g), w2k_ref[g], preferred_element_type=F32)
        vt_out = vt_out + _nt(w2vt_ref[g], hidden(1, g))
    k_ref[0] = k_out.astype(BF16)
    vt_ref[0] = vt_out.astype(BF16)


def _compress(chunks, w1, pe, w2k, w2vt):
    bsz, nrow, width = chunks[0].shape
    const = lambda a: pl.BlockSpec(a.shape, lambda b: (0,) * a.ndim)
    chunk_spec = pl.BlockSpec((1, nrow, width), lambda b: (b, 0, 0))
    return pl.pallas_call(
        _compress_kernel,
        grid=(bsz,),
        in_specs=[chunk_spec] * 4 + [const(w1), const(pe), const(w2k), const(w2vt)],
        out_specs=[pl.BlockSpec((1, nrow, LANES), lambda b: (b, 0, 0)),
                   pl.BlockSpec((1, LANES, nrow), lambda b: (b, 0, 0))],
        out_shape=[jax.ShapeDtypeStruct((bsz, nrow, LANES), BF16),
                   jax.ShapeDtypeStruct((bsz, LANES, nrow), BF16)],
        compiler_params=pltpu.CompilerParams(dimension_semantics=("arbitrary",),
                                             vmem_limit_bytes=VMEM_LIMIT),
        name="compress",
    )(*chunks, w1, pe, w2k, w2vt)


NSA_ROWS = NSA_GROUP * SEL_BLOCK
NSA_LANES = NSA_KV_HEADS * NSA_ROWS
SEL_CHUNK = VT_CHUNK
WIN_KEYS = WINDOW + VW_CHUNK
ONES_ROWS = 16


def _with_ones(vt):
    return jnp.concatenate([vt, jnp.ones((ONES_ROWS, vt.shape[1]), vt.dtype)], axis=0)


def _nsa_kernel(qr_ref, qc_ref, ks_ref, oh_ref, vs_ref, kw_ref, vw_ref, kc_ref, vct_ref, gate_ref,
                ovt_ref, dtab_ref, ctab_ref, o_ref, m_ref, acc_ref, s_ref):
    i = pl.program_id(1)
    t0 = i * SEL_BLOCK

    def stack_heads(ref):
        return jnp.concatenate([ref[0, :, h * LANES:(h + 1) * LANES] for h in range(NSA_HEADS)], axis=0)

    def pv(vts, p):
        return jnp.concatenate(
            [jnp.dot(_with_ones(vts[g * HEAD_DIM:(g + 1) * HEAD_DIM]), p[:, g * NSA_ROWS:(g + 1) * NSA_ROWS],
                     preferred_element_type=F32) for g in range(NSA_KV_HEADS)], axis=1)

    lane128 = lax.broadcasted_iota(jnp.int32, (SEL_BLOCK, LANES), 1)
    tok128 = lax.broadcasted_iota(jnp.int32, (SEL_BLOCK, LANES), 0)
    lower = lane128 < HEAD_DIM
    gates = gate_ref[0, 0]

    qc = stack_heads(qc_ref)
    ctab = ctab_ref[...]
    any_valid = jnp.where(ctab[0:1] <= t0, 1.0, 0.0)
    s = jnp.where(ctab <= t0, _nt(kc_ref[0], qc), NEG)
    e = jnp.exp(s - jnp.max(s, axis=0, keepdims=True))
    e_hi, e_lo = _split_bf16(e)
    acc = pv(vct_ref[0], e_hi)
    inv = any_valid / acc[HEAD_DIM:HEAD_DIM + 1]
    out = (gates[0:1] * inv) * acc[0:HEAD_DIM]
    imp4 = (jnp.dot(ovt_ref[...], e_hi, preferred_element_type=F32)
            + jnp.dot(ovt_ref[...], e_lo, preferred_element_type=F32)) * inv
    imps = []
    for g in range(NSA_KV_HEADS):
        two = imp4[:, 2 * g * LANES:(2 * g + 1) * LANES] + imp4[:, (2 * g + 1) * LANES:(2 * g + 2) * LANES]
        imps.append(two + pltpu.roll(two, HEAD_DIM, 1))
    imp_t = jnp.where(lower, imps[0], imps[1])

    qr = stack_heads(qr_ref)
    nwin = WIN_KEYS // VW_CHUNK
    wb0 = jnp.maximum(i // (VW_CHUNK // SEL_BLOCK) - WINDOW // VW_CHUNK, 0)
    wrows = pl.ds(pl.multiple_of(wb0 * VW_CHUNK, VW_CHUNK), WIN_KEYS)
    dist = (t0 - wb0 * VW_CHUNK) - dtab_ref[...]
    s = jnp.where(lax.bitcast_convert_type(dist, jnp.uint32) < WINDOW, _nt(kw_ref[0, wrows, :], qr), NEG)
    e = jnp.exp(s - jnp.max(s, axis=0, keepdims=True)).astype(BF16)
    acc = pv(jnp.concatenate([vw_ref[wb0 + n] for n in range(nwin)], axis=1), e)
    out = out + (gates[2:3] * (1.0 / acc[HEAD_DIM:HEAD_DIM + 1])) * acc[0:HEAD_DIM]

    nb_pad = imp_t.shape[0]
    jj = tok128
    forced = (jj == 0) | (jj == i) | (jj == i - 1)
    val = jnp.where(jj > i, -jnp.inf, jnp.where(forced, FORCE_SCORE, imp_t))
    sub = 8
    blocks = [val[v * sub:(v + 1) * sub, :] for v in range(nb_pad // sub)]
    ranks = [jnp.zeros((sub, LANES), F32) for _ in blocks]
    jloc = lax.broadcasted_iota(jnp.int32, (sub, LANES), 0)
    for k in range(nb_pad):
        rowk = jnp.broadcast_to(val[k:k + 1, :], (sub, LANES))
        for v in range(nb_pad // sub):
            if v * sub > k:
                beats = jnp.where(rowk >= blocks[v], 1.0, 0.0)
            elif (v + 1) * sub <= k:
                beats = jnp.where(rowk > blocks[v], 1.0, 0.0)
            else:
                beats = jnp.where(jloc > k - v * sub,
                                  jnp.where(rowk >= blocks[v], 1.0, 0.0),
                                  jnp.where(rowk > blocks[v], 1.0, 0.0))
            ranks[v] = ranks[v] + beats
    rank = jnp.concatenate(ranks, axis=0)
    sel_bias_t = jnp.where(rank < float(SEL_TOPK), jnp.where(jj <= i, 0.0, NEG), NEG).astype(BF16)
    xg = jnp.concatenate([sel_bias_t, jnp.zeros_like(sel_bias_t)], axis=0)
    bias_rows = []
    for g in range(NSA_KV_HEADS):
        pick = jnp.where(lane128 == tok128 + g * SEL_BLOCK, 1.0, 0.0).astype(BF16)
        bias_rows += [_nt(pick, xg).astype(BF16)] * NSA_GROUP
    q_aug = jnp.concatenate([qr, jnp.concatenate(bias_rows, axis=0)], axis=1)

    m_ref[...] = jnp.full(m_ref.shape, NEG, F32)
    acc_ref[...] = jnp.zeros(acc_ref.shape, F32)
    per_chunk = SEL_CHUNK // SEL_BLOCK
    nfull = i // per_chunk

    def scores(c, slot):
        rows = pl.ds(pl.multiple_of(c * SEL_CHUNK, SEL_CHUNK), SEL_CHUNK)
        k_aug = jnp.concatenate([ks_ref[0, rows, :], oh_ref[rows, :]], axis=1)
        s_ref[slot] = _nt(k_aug, q_aug)

    def absorb(c, slot, diag):
        s = s_ref[slot]
        if diag:
            s = jnp.where(dtab_ref[0:SEL_CHUNK, :] > (i % per_chunk) * SEL_BLOCK, NEG, s)
        m_old = m_ref[...]
        m_new = jnp.maximum(m_old, jnp.max(s, axis=0, keepdims=True))
        p = jnp.exp(s - m_new).astype(BF16)
        acc_ref[...] = jnp.exp(m_old - m_new) * acc_ref[...] + pv(vs_ref[c], p)
        m_ref[...] = m_new

    scores(0, 0)

    def pair(jp, carry):
        c = 2 * jp
        scores(c + 1, 1)
        absorb(c, 0, False)
        scores(c + 2, 0)
        absorb(c + 1, 1, False)
        return carry

    lax.fori_loop(0, nfull // 2, pair, 0)

    @pl.when(nfull % 2 == 1)
    def _():
        scores(nfull, 1)
        absorb(nfull - 1, 0, False)
        absorb(nfull, 1, True)

    @pl.when(nfull % 2 == 0)
    def _():
        absorb(nfull, 0, True)

    acc = acc_ref[...]
    out = (out + (gates[1:2] * (1.0 / acc[HEAD_DIM:HEAD_DIM + 1])) * acc[0:HEAD_DIM]).astype(BF16)

    fold = jnp.where((lane128 & (SEL_BLOCK - 1)) == tok128, 1.0, 0.0).astype(BF16)
    zero = jnp.zeros((HEAD_DIM, LANES), BF16)
    for k in range(NSA_HEADS // 2):
        x = out[:, k * LANES:(k + 1) * LANES]
        y = jnp.concatenate([jnp.where(lower, x, zero), jnp.where(lower, zero, x)], axis=0)
        o_ref[0, :, k * LANES:(k + 1) * LANES] = _nt(fold, y).astype(BF16)


def _nsa(proj3, gates_t, onehot, vt512, vw128, kcmp, vct, ovt, dtab, ctab):
    bsz, seq, _ = proj3.shape
    nblk = seq // SEL_BLOCK
    qw = NSA_HEADS * LANES
    const = lambda a: pl.BlockSpec(a.shape, lambda b, i: (0,) * a.ndim)
    per_b = lambda a: pl.BlockSpec((1,) + a.shape[1:], lambda b, i: (b,) + (0,) * (a.ndim - 1))
    return pl.pallas_call(
        _nsa_kernel,
        grid=(bsz, nblk),
        in_specs=[pl.BlockSpec((1, SEL_BLOCK, qw), lambda b, i: (b, i, COL_QR // qw)),
                  pl.BlockSpec((1, SEL_BLOCK, qw), lambda b, i: (b, i, COL_QC // qw)),
                  pl.BlockSpec((1, seq, LANES), lambda b, i: (b, 0, COL_KS // LANES)),
                  const(onehot),
                  pl.BlockSpec((seq // VT_CHUNK, LANES, VT_CHUNK), lambda b, i: (b, ROW_VS // LANES, 0)),
                  pl.BlockSpec((1, seq, LANES), lambda b, i: (b, 0, COL_KW // LANES)),
                  pl.BlockSpec((seq // VW_CHUNK, LANES, VW_CHUNK), lambda b, i: (b, 0, 0)),
                  per_b(kcmp), per_b(vct),
                  pl.BlockSpec((1, 1) + gates_t.shape[2:], lambda b, i: (b, i, 0, 0)),
                  const(ovt), const(dtab), const(ctab)],
        out_specs=pl.BlockSpec((1, SEL_BLOCK, NSA_HEADS * HEAD_DIM), lambda b, i: (b, i, 0)),
        out_shape=jax.ShapeDtypeStruct((bsz, seq, NSA_HEADS * HEAD_DIM), BF16),
        scratch_shapes=[pltpu.VMEM((1, NSA_LANES), F32), pltpu.VMEM((HEAD_DIM + ONES_ROWS, NSA_LANES), F32),
                        pltpu.VMEM((2, SEL_CHUNK, NSA_LANES), F32)],
        compiler_params=pltpu.CompilerParams(dimension_semantics=("arbitrary", "arbitrary"),
                                             vmem_limit_bytes=VMEM_LIMIT),
        name="nsa",
    )(proj3, proj3, proj3, onehot, vt512, proj3, vw128, kcmp, vct, gates_t, ovt, dtab, ctab)


DIFF_TQ = VT_CHUNK


def _diff_kernel(q_ref, k_ref, vt_ref, lam_ref, ng_ref, o_ref, m_ref, acc_ref, s_ref, *, lambda_init):
    i = pl.program_id(2)
    tq = DIFF_TQ
    nv = 2 * HEAD_DIM
    q = q_ref[0]
    lane = lax.broadcasted_iota(jnp.int32, (tq, LANES), 1)
    zero = jnp.zeros_like(q)
    q_both = jnp.concatenate([jnp.where(lane < HEAD_DIM, q, zero), jnp.where(lane >= HEAD_DIM, q, zero)], axis=0)
    m_ref[...] = jnp.full(m_ref.shape, NEG, F32)
    acc_ref[...] = jnp.zeros(acc_ref.shape, F32)

    def scores(c, slot):
        k = k_ref[0, pl.ds(pl.multiple_of(c * tq, tq), tq), :]
        s_ref[slot] = _nt(k, q_both)

    def absorb(c, slot, causal):
        s = s_ref[slot]
        if causal:
            kpos = lax.broadcasted_iota(jnp.int32, (tq, 2 * tq), 0)
            qpos = lax.broadcasted_iota(jnp.int32, (tq, 2 * tq), 1) & (tq - 1)
            s = jnp.where(kpos <= qpos, s, NEG)
        m_old = m_ref[...]
        m_new = jnp.maximum(m_old, jnp.max(s, axis=0, keepdims=True))
        p = jnp.exp(s - m_new).astype(BF16)
        acc_ref[...] = jnp.exp(m_old - m_new) * acc_ref[...] + jnp.dot(_with_ones(vt_ref[c]), p,
                                                                       preferred_element_type=F32)
        m_ref[...] = m_new

    scores(0, 0)

    def pair(jp, carry):
        c = 2 * jp
        scores(c + 1, 1)
        absorb(c, 0, False)
        scores(c + 2, 0)
        absorb(c + 1, 1, False)
        return carry

    lax.fori_loop(0, i // 2, pair, 0)

    @pl.when(i % 2 == 1)
    def _():
        scores(i, 1)
        absorb(i - 1, 0, False)
        absorb(i, 1, True)

    @pl.when(i % 2 == 0)
    def _():
        absorb(i, 0, True)

    lv = lam_ref[...]
    lam = (jnp.exp(jnp.sum(lv[0:1] * lv[1:2], axis=1, keepdims=True))
           - jnp.exp(jnp.sum(lv[2:3] * lv[3:4], axis=1, keepdims=True)) + lambda_init)
    acc = acc_ref[...]
    on = acc[0:nv] * (1.0 / acc[nv:nv + 1])
    o = on[:, 0:tq] - lam * on[:, tq:2 * tq]
    o = o * lax.rsqrt(jnp.mean(o * o, axis=0, keepdims=True) + EPS) * ng_ref[...] * (1.0 - lambda_init)
    o_ref[0] = o.T.astype(BF16)


def _diff(proj3, vt512, lam_vec, norm_g, lambda_init):
    bsz, seq, _ = proj3.shape
    tq = DIFF_TQ
    kern = functools.partial(_diff_kernel, lambda_init=lambda_init)
    ng = jnp.broadcast_to(norm_g.reshape(LANES, 1), (LANES, tq))
    return pl.pallas_call(
        kern,
        grid=(bsz, DIFF_HEADS, seq // tq),
        in_specs=[pl.BlockSpec((1, tq, LANES), lambda b, h, i: (b, i, COL_QD // LANES + h)),
                  pl.BlockSpec((1, seq, LANES), lambda b, h, i: (b, 0, COL_KD // LANES + h)),
                  pl.BlockSpec((seq // VT_CHUNK, LANES, VT_CHUNK), lambda b, h, i: (b, ROW_VD // LANES + h, 0)),
                  pl.BlockSpec(lam_vec.shape, lambda b, h, i: (0, 0)),
                  pl.BlockSpec((LANES, tq), lambda b, h, i: (0, 0))],
        out_specs=pl.BlockSpec((1, tq, LANES), lambda b, h, i: (b, i, h)),
        out_shape=jax.ShapeDtypeStruct((bsz, seq, DIFF_HEADS * LANES), BF16),
        scratch_shapes=[pltpu.VMEM((1, 2 * tq), F32), pltpu.VMEM((2 * HEAD_DIM + ONES_ROWS, 2 * tq), F32),
                        pltpu.VMEM((2, tq, 2 * tq), F32)],
        compiler_params=pltpu.CompilerParams(dimension_semantics=("arbitrary", "arbitrary", "arbitrary"),
                                             vmem_limit_bytes=VMEM_LIMIT),
        name="diff",
    )(proj3, proj3, vt512, lam_vec, ng)


def _merge_kernel(x_ref, ada_ref, on_ref, od_ref, wg_ref, wbn_ref, wbd_ref, wo_ref, g_ref, b_ref, o_ref, *, alpha):
    d = x_ref.shape[1]
    x = x_ref[...]
    sh = ada_ref[0, 0:1, :]
    sc = ada_ref[0, 1:2, :]
    gate = ada_ref[0, 2:3, :]
    u = (x * (1.0 + sc) + sh).astype(BF16)
    gm = jax.nn.sigmoid(jnp.dot(u, wg_ref[...], preferred_element_type=F32))
    y_nsa = jnp.dot(on_ref[...], wbn_ref[...], preferred_element_type=F32)
    y_diff = jnp.dot(od_ref[...], wbd_ref[...], preferred_element_type=F32)
    mixed = (gm[:, 0:d] * y_nsa + gm[:, d:2 * d] * y_diff).astype(BF16)
    mix = jnp.dot(mixed, wo_ref[...], preferred_element_type=F32)
    o_ref[...] = _layer_norm(alpha * x + (1.0 + gate) * mix, g_ref[...], b_ref[...])


def _merge(x2, ada3, o_nsa, o_diff, w_gm, w_bn, w_bd, w_o, ln_g, ln_b, seq, alpha):
    m, d = x2.shape
    tm = min(512, seq)
    per_seq = seq // tm
    full = lambda a: pl.BlockSpec(a.shape, lambda i: (0,) * a.ndim)
    kern = functools.partial(_merge_kernel, alpha=alpha)
    return pl.pallas_call(
        kern,
        grid=(m // tm,),
        in_specs=[pl.BlockSpec((tm, d), lambda i: (i, 0)),
                  pl.BlockSpec((1, 6, d), lambda i: (i // per_seq, 0, 0)),
                  pl.BlockSpec((tm, o_nsa.shape[1]), lambda i: (i, 0)),
                  pl.BlockSpec((tm, o_diff.shape[1]), lambda i: (i, 0)),
                  full(w_gm), full(w_bn), full(w_bd), full(w_o),
                  pl.BlockSpec((1, d), lambda i: (0, 0)),
                  pl.BlockSpec((1, d), lambda i: (0, 0))],
        out_specs=pl.BlockSpec((tm, d), lambda i: (i, 0)),
        out_shape=jax.ShapeDtypeStruct((m, d), F32),
        compiler_params=pltpu.CompilerParams(dimension_semantics=("arbitrary",),
                                             vmem_limit_bytes=VMEM_LIMIT),
        name="merge",
    )(x2, ada3, o_nsa, o_diff, w_gm, w_bn, w_bd, w_o, ln_g.reshape(1, d), ln_b.reshape(1, d))


def _ffn_kernel(x_ref, ada_ref, wab_ref, wo_ref, g_ref, b_ref, o_ref, u_ref, acc_ref, *, alpha):
    f = pl.program_id(1)

    @pl.when(f == 0)
    def _():
        sh = ada_ref[0, 3:4, :]
        sc = ada_ref[0, 4:5, :]
        u_ref[...] = (x_ref[...] * (1.0 + sc) + sh).astype(BF16)
        acc_ref[...] = jnp.zeros(acc_ref.shape, F32)

    u = u_ref[...]
    ab = jnp.dot(u, wab_ref[...], preferred_element_type=F32)
    tf = wo_ref.shape[0]
    a = ab[:, 0:tf]
    b = ab[:, tf:2 * tf]
    h = ((a * jax.nn.sigmoid(a)) * b).astype(BF16)
    acc_ref[...] += jnp.dot(h, wo_ref[...], preferred_element_type=F32)

    @pl.when(f == pl.num_programs(1) - 1)
    def _():
        gate = ada_ref[0, 5:6, :]
        o_ref[...] = _layer_norm(alpha * x_ref[...] + (1.0 + gate) * acc_ref[...], g_ref[...], b_ref[...])


def _ffn(x2, ada3, w_in, w_out, ln_g, ln_b, seq, alpha):
    m, d = x2.shape
    dff = w_out.shape[0]
    tm = min(512, seq)
    tf = dff // 2
    nf = dff // tf
    per_seq = seq // tm
    w_ab = jnp.concatenate([w_in[:, half * dff + f * tf:half * dff + (f + 1) * tf]
                            for f in range(nf) for half in range(2)], axis=1)
    kern = functools.partial(_ffn_kernel, alpha=alpha)
    return pl.pallas_call(
        kern,
        grid=(m // tm, nf),
        in_specs=[pl.BlockSpec((tm, d), lambda i, f: (i, 0)),
                  pl.BlockSpec((1, 6, d), lambda i, f: (i // per_seq, 0, 0)),
                  pl.BlockSpec((d, 2 * tf), lambda i, f: (0, f)),
                  pl.BlockSpec((tf, d), lambda i, f: (f, 0)),
                  pl.BlockSpec((1, d), lambda i, f: (0, 0)),
                  pl.BlockSpec((1, d), lambda i, f: (0, 0))],
        out_specs=pl.BlockSpec((tm, d), lambda i, f: (i, 0)),
        out_shape=jax.ShapeDtypeStruct((m, d), F32),
        scratch_shapes=[pltpu.VMEM((tm, d), BF16), pltpu.VMEM((tm, d), F32)],
        compiler_params=pltpu.CompilerParams(dimension_semantics=("arbitrary", "arbitrary"),
                                             vmem_limit_bytes=VMEM_LIMIT),
        name="ffn",
    )(x2, ada3, w_ab, w_out, ln_g.reshape(1, d), ln_b.reshape(1, d))


def _rope_tables(seq):
    half = ROPE_DIMS // 2
    inv_freq = ROPE_THETA ** (-jnp.arange(half, dtype=F32) * 2.0 / ROPE_DIMS)
    ang = jnp.arange(seq, dtype=F32)[:, None] * inv_freq[None, :]
    cos, sin = jnp.cos(ang), jnp.sin(ang)
    ones = jnp.ones((seq, HEAD_DIM - ROPE_DIMS), F32)
    zeros = jnp.zeros((seq, HEAD_DIM - ROPE_DIMS), F32)
    z8 = jnp.zeros((seq, half), F32)
    cos_h = jnp.concatenate([cos, cos, ones], axis=1)
    sa_h = jnp.concatenate([-sin, z8, zeros], axis=1)
    sb_h = jnp.concatenate([z8, sin, zeros], axis=1)
    rep = lambda t: jnp.concatenate([t] * (LANES // HEAD_DIM), axis=1)
    return rep(cos_h), rep(sa_h), rep(sb_h)


def _mask_tables(seq):
    nb = seq // SEL_BLOCK
    nc = (seq - CMP_BLOCK) // CMP_STRIDE + 1
    ncp = seq // CMP_STRIDE
    cs = np.arange(nc) * CMP_STRIDE
    bs = np.arange(nb) * SEL_BLOCK
    ov = np.minimum(cs[:, None] + CMP_BLOCK, bs[None, :] + SEL_BLOCK) - np.maximum(cs[:, None], bs[None, :])
    ov = np.clip(ov, 0, None) / CMP_BLOCK
    ovt = np.zeros((SEL_BLOCK, ncp), np.float32)
    ovt[:nb, :nc] = ov.T
    tl = np.arange(NSA_LANES) % SEL_BLOCK
    dtab = (np.arange(WIN_KEYS)[:, None] - tl[None, :]).astype(np.int32)
    ctab = (np.arange(ncp)[:, None] * CMP_STRIDE + (CMP_BLOCK - 1) - tl[None, :]).astype(np.int32)
    return jnp.asarray(ovt, BF16), jnp.asarray(dtab), jnp.asarray(ctab)


def _proj_weight(w_in_l):
    d = w_in_l.shape[0]
    sizes = (512, 128, 128, 128, 128, 128, 128, 24, 512, 512, 512, 2 * d)
    offs = np.cumsum((0,) + sizes)
    q_n, kc, vc, ks, vs, kw, vw, g_n, q_d, k_d, v_d, g_m = [w_in_l[:, offs[k]:offs[k + 1]] for k in range(12)]
    qh = q_n.reshape(d, NSA_HEADS, HEAD_DIM)
    z = jnp.zeros_like(qh)
    first = jnp.arange(NSA_HEADS)[None, :, None] < NSA_GROUP
    q_wide = jnp.concatenate([jnp.where(first, qh, z), jnp.where(first, z, qh)], axis=-1).reshape(d, NSA_HEADS * LANES)
    zeros = lambda n: jnp.zeros((d, n), w_in_l.dtype)
    half_slabs = [jnp.concatenate([t[:, g * HEAD_DIM:(g + 1) * HEAD_DIM], zeros(LANES - HEAD_DIM)], axis=1)
                  for t in (kc, vc) for g in range(NSA_KV_HEADS)]
    w = jnp.concatenate([q_wide, q_wide, q_d, k_d, ks, kw, g_n, zeros(COL_KV - COL_GN - g_n.shape[1])]
                        + half_slabs, axis=1)
    assert w.shape[1] == PROJ_COLS
    wtv = jnp.concatenate([v_d, vs, vw], axis=1).T
    scale = np.ones((1, PROJ_COLS), np.float32)
    scale[:, COL_QR:COL_QD + 512] = QK_SCALE
    return w.astype(BF16), wtv.astype(BF16), jnp.asarray(scale), g_m.astype(BF16)


def kernel(x, c, w_ada, b_ada, w_in, cmp_pe_k, cmp_w1_k, cmp_w2_k, cmp_pe_v, cmp_w1_v, cmp_w2_v, diff_lambda, diff_norm_g, w_branch_nsa, w_branch_diff, w_out, ln1_g, ln1_b, w_ffn_in, w_ffn_out, ln2_g, ln2_b):
    bsz, seq, d = x.shape
    depth = w_ada.shape[0]
    assert seq % 1024 == 0, seq
    assert seq // SEL_BLOCK <= SEL_BLOCK
    alpha = (2 * depth) ** 0.25
    ncp = seq // CMP_STRIDE
    nblk = seq // SEL_BLOCK
    cos_t, sa_t, sb_t = _rope_tables(seq)
    ovt, dtab, ctab = _mask_tables(seq)
    onehot = (jnp.arange(seq)[:, None] // SEL_BLOCK == jnp.arange(LANES)[None, :]).astype(BF16)

    x2 = x.reshape(bsz * seq, d)
    for l in range(depth):
        lambda_init = 0.8 - 0.6 * math.exp(-0.3 * l)
        ada3 = _ada(c, w_ada[l], b_ada[l]).reshape(bsz, 6, d)
        w_proj, wtv, colscale, w_gm = _proj_weight(w_in[l])
        proj, gates, vt512, vw128, *kv_raw = _proj(x2, ada3, w_proj, wtv, colscale, cos_t, sa_t, sb_t, seq)
        proj3 = proj.reshape(bsz, seq, PROJ_COLS)
        chunks = [t.reshape(bsz, ncp, CMP_STRIDE * HEAD_DIM) for t in kv_raw]
        w1 = jnp.stack([cmp_w1_k[l], cmp_w1_v[l]]).astype(BF16)
        pe = jnp.stack([cmp_pe_k[l], cmp_pe_v[l]]).reshape(2, 1, CMP_BLOCK * HEAD_DIM)
        pe = jnp.broadcast_to(pe, (2, 8, CMP_BLOCK * HEAD_DIM)).astype(BF16)
        w2k = cmp_w2_k[l].astype(BF16)
        w2v = cmp_w2_v[l].astype(BF16).T
        w2k = jnp.stack([jnp.concatenate([w2k, jnp.zeros_like(w2k)], axis=1),
                         jnp.concatenate([jnp.zeros_like(w2k), w2k], axis=1)])
        w2vt = jnp.stack([jnp.concatenate([w2v, jnp.zeros_like(w2v)], axis=0),
                          jnp.concatenate([jnp.zeros_like(w2v), w2v], axis=0)])
        kcmp, vct = _compress(chunks, w1, pe, w2k, w2vt)

        gates_t = gates.reshape(bsz, nblk, SEL_BLOCK, LANES)[..., :NSA_HEADS * 3]
        gates_t = gates_t.reshape(bsz, nblk, SEL_BLOCK, NSA_HEADS, 3)
        gates_t = gates_t.transpose(0, 1, 4, 3, 2).reshape(bsz, nblk, 3, NSA_LANES)
        gates_t = jnp.pad(gates_t, ((0, 0), (0, 0), (0, 8 - 3), (0, 0)))
        o_nsa = _nsa(proj3, gates_t, onehot, vt512, vw128, kcmp, vct, ovt, dtab, ctab)
        o_diff = _diff(proj3, vt512, diff_lambda[l], diff_norm_g[l], lambda_init)

        x2 = _merge(x2, ada3, o_nsa.reshape(bsz * seq, -1), o_diff.reshape(bsz * seq, -1), w_gm,
                    w_branch_nsa[l].astype(BF16), w_branch_diff[l].astype(BF16), w_out[l].astype(BF16),
                    ln1_g[l], ln1_b[l], seq, alpha)
        x2 = _ffn(x2, ada3, w_ffn_in[l].astype(BF16), w_ffn_out[l].astype(BF16), ln2_g[l], ln2_b[l], seq, alpha)
    return x2.reshape(bsz, seq, d)
```

```python
import functools
import math

import numpy as np
import jax
import jax.numpy as jnp
from jax import lax
from jax.experimental import pallas as pl
from jax.experimental.pallas import tpu as pltpu

F32 = jnp.float32
BF16 = jnp.bfloat16

HEAD_DIM = 64
NSA_HEADS = 8
NSA_KV_HEADS = 2
NSA_GROUP = NSA_HEADS // NSA_KV_HEADS
CMP_BLOCK = 32
CMP_STRIDE = 16
CMP_HIDDEN = 256
SEL_BLOCK = 64
SEL_TOPK = 16
WINDOW = 512
DIFF_HEADS = 4
ROPE_THETA = 500000.0
ROPE_DIMS = HEAD_DIM // 4
EPS = 1e-5
NEG = -1e30
FORCE_SCORE = 1e9
QK_SCALE = HEAD_DIM ** -0.5

LANES = 128
VMEM_LIMIT = 56 * 1024 * 1024

PROJ_TN = 1024
MXU_COLS = 256
COL_QR = 0
COL_QC = 1024
COL_QD = 2048
COL_KD = 2560
COL_KS = 3072
COL_KW = 3200
COL_GN = 3328
COL_KV = 3584
PROJ_COLS = 4096
ROPE_TILES = (0, 2)
PLAIN_TILES = (1,)
LAST_TILE = 3
ROW_VD = 0
ROW_VS = 512
ROW_VW = 640
VT_ROWS_ALL = 768
VT_CHUNK = 512
VW_CHUNK = 128


def _nt(a, b):
    return lax.dot_general(a, b, (((1,), (1,)), ((), ())), preferred_element_type=F32)


def _split_bf16(a):
    hi = a.astype(BF16)
    lo = (a - hi.astype(F32)).astype(BF16)
    return hi, lo


def _layer_norm(h, g, b):
    mu = jnp.mean(h, axis=-1, keepdims=True)
    d = h - mu
    var = jnp.mean(d * d, axis=-1, keepdims=True)
    return d * lax.rsqrt(var + EPS) * g + b


def _ada_kernel(c_ref, w_ref, b_ref, o_ref):
    c = c_ref[...]
    a = c * jax.nn.sigmoid(c)
    a_hi, a_lo = _split_bf16(a)
    w_hi, w_lo = _split_bf16(w_ref[...])
    acc = jnp.dot(a_hi, w_hi, preferred_element_type=F32)
    acc += jnp.dot(a_lo, w_hi, preferred_element_type=F32)
    acc += jnp.dot(a_hi, w_lo, preferred_element_type=F32)
    o_ref[...] = acc + b_ref[...]


def _ada(c, w, b):
    bsz, d = c.shape
    n = w.shape[1]
    tn = 1024
    return pl.pallas_call(
        _ada_kernel,
        grid=(n // tn,),
        in_specs=[pl.BlockSpec((bsz, d), lambda j: (0, 0)),
                  pl.BlockSpec((d, tn), lambda j: (0, j)),
                  pl.BlockSpec((1, tn), lambda j: (0, j))],
        out_specs=pl.BlockSpec((bsz, tn), lambda j: (0, j)),
        out_shape=jax.ShapeDtypeStruct((bsz, n), F32),
        compiler_params=pltpu.CompilerParams(dimension_semantics=("arbitrary",),
                                             vmem_limit_bytes=VMEM_LIMIT),
        name="ada",
    )(c, w, b.reshape(1, n))


def _proj_kernel(x_ref, ada_ref, w_ref, wtv_ref, cs_ref, cos_ref, sa_ref, sb_ref,
                 o_ref, g_ref, vt_ref, vw_ref, kc0_ref, kc1_ref, vc0_ref, vc1_ref, u_ref):
    j = pl.program_id(1)
    tm = u_ref.shape[0]
    tn = w_ref.shape[1]

    @pl.when(j == 0)
    def _():
        sh = ada_ref[0, 0:1, :]
        sc = ada_ref[0, 1:2, :]
        u_ref[...] = (x_ref[...] * (1.0 + sc) + sh).astype(BF16)
        vt = _nt(wtv_ref[...], u_ref[...]).astype(BF16)
        for cc in range(tm // VT_CHUNK):
            vt_ref[cc] = vt[ROW_VD:ROW_VW, cc * VT_CHUNK:(cc + 1) * VT_CHUNK]
        for cc in range(tm // VW_CHUNK):
            vw_ref[cc] = vt[ROW_VW:VT_ROWS_ALL, cc * VW_CHUNK:(cc + 1) * VW_CHUNK]

    def col_tile(ct):
        cols = slice(ct * MXU_COLS, (ct + 1) * MXU_COLS)
        return jnp.dot(u_ref[...], w_ref[:, cols], preferred_element_type=F32) * cs_ref[:, cols]

    def store_rope(a, ct):
        cosv, sav, sbv = cos_ref[...], sa_ref[...], sb_ref[...]
        for s in range(MXU_COLS // LANES):
            a_s = a[:, s * LANES:(s + 1) * LANES]
            r = a_s * cosv + pltpu.roll(a_s, LANES - ROPE_DIMS // 2, 1) * sav + pltpu.roll(a_s, ROPE_DIMS // 2, 1) * sbv
            o_ref[:, ct * MXU_COLS + s * LANES:ct * MXU_COLS + (s + 1) * LANES] = r.astype(BF16)

    def store_plain(a, ct):
        o_ref[:, ct * MXU_COLS:(ct + 1) * MXU_COLS] = a.astype(BF16)

    def any_of(tiles):
        return functools.reduce(jnp.logical_or, [j == t for t in tiles])

    @pl.when(any_of(ROPE_TILES))
    def _():
        for ct in range(tn // MXU_COLS):
            store_rope(col_tile(ct), ct)

    @pl.when(any_of(PLAIN_TILES))
    def _():
        for ct in range(tn // MXU_COLS):
            store_plain(col_tile(ct), ct)

    @pl.when(j == LAST_TILE)
    def _():
        gate_ct = (COL_GN - LAST_TILE * PROJ_TN) // MXU_COLS
        kv_ct = (COL_KV - LAST_TILE * PROJ_TN) // MXU_COLS
        kv_refs = ((kc0_ref, kc1_ref), (vc0_ref, vc1_ref))
        for ct in range(tn // MXU_COLS):
            a = col_tile(ct)
            if ct < kv_ct:
                if ct == gate_ct:
                    g_ref[...] = jax.nn.sigmoid(a[:, 0:LANES])
                store_rope(a, ct)
            else:
                for g, ref in enumerate(kv_refs[ct - kv_ct]):
                    ref[...] = a[:, g * LANES:g * LANES + HEAD_DIM].astype(BF16)
                store_plain(a, ct)


def _proj(x2, ada3, w, wtv, colscale, cos_t, sa_t, sb_t, seq):
    m, d = x2.shape
    tm = min(1024, seq)
    tn = PROJ_TN
    per_seq = seq // tm
    kv_spec = pl.BlockSpec((tm, HEAD_DIM), lambda i, j: (i, 0))
    kv_shape = jax.ShapeDtypeStruct((m, HEAD_DIM), BF16)
    return pl.pallas_call(
        _proj_kernel,
        grid=(m // tm, w.shape[1] // tn),
        in_specs=[pl.BlockSpec((tm, d), lambda i, j: (i, 0)),
                  pl.BlockSpec((1, 6, d), lambda i, j: (i // per_seq, 0, 0)),
                  pl.BlockSpec((d, tn), lambda i, j: (0, j)),
                  pl.BlockSpec(wtv.shape, lambda i, j: (0, 0)),
                  pl.BlockSpec((1, tn), lambda i, j: (0, j)),
                  pl.BlockSpec((tm, LANES), lambda i, j: (i % per_seq, 0)),
                  pl.BlockSpec((tm, LANES), lambda i, j: (i % per_seq, 0)),
                  pl.BlockSpec((tm, LANES), lambda i, j: (i % per_seq, 0))],
        out_specs=[pl.BlockSpec((tm, tn), lambda i, j: (i, j)),
                   pl.BlockSpec((tm, LANES), lambda i, j: (i, 0)),
                   pl.BlockSpec((tm // VT_CHUNK, ROW_VW, VT_CHUNK), lambda i, j: (i, 0, 0)),
                   pl.BlockSpec((tm // VW_CHUNK, VT_ROWS_ALL - ROW_VW, VW_CHUNK), lambda i, j: (i, 0, 0)),
                   kv_spec, kv_spec, kv_spec, kv_spec],
        out_shape=[jax.ShapeDtypeStruct((m, PROJ_COLS), BF16),
                   jax.ShapeDtypeStruct((m, LANES), F32),
                   jax.ShapeDtypeStruct((m // VT_CHUNK, ROW_VW, VT_CHUNK), BF16),
                   jax.ShapeDtypeStruct((m // VW_CHUNK, VT_ROWS_ALL - ROW_VW, VW_CHUNK), BF16),
                   kv_shape, kv_shape, kv_shape, kv_shape],
        scratch_shapes=[pltpu.VMEM((tm, d), BF16)],
        compiler_params=pltpu.CompilerParams(dimension_semantics=("arbitrary", "arbitrary"),
                                             vmem_limit_bytes=VMEM_LIMIT),
        name="proj",
    )(x2, ada3, w, wtv, colscale, cos_t, sa_t, sb_t)


def _gelu_tanh(x):
    return x * (0.5 * (1.0 + jnp.tanh(math.sqrt(2.0 / math.pi) * (x + 0.044715 * (x * x * x)))))


def _compress_kernel(kc0_ref, kc1_ref, vc0_ref, vc1_ref, w1_ref, pe_ref, w2k_ref, w2vt_ref, k_ref, vt_ref):
    xs = ((kc0_ref, kc1_ref), (vc0_ref, vc1_ref))
    nrow = kc0_ref.shape[1]
    half = w1_ref.shape[1] // 2

    def hidden(which, g):
        w1 = w1_ref[which]
        pe_row = jnp.dot(pe_ref[which], w1, preferred_element_type=F32)[0:1, :]
        x = xs[which][g][0]
        a = jnp.dot(x, w1[0:half], preferred_element_type=F32)
        b = jnp.dot(x, w1[half:2 * half], preferred_element_type=F32)
        return _gelu_tanh(a + pltpu.roll(b, nrow - 1, 0) + pe_row).astype(BF16)

    k_out = jnp.zeros((nrow, LANES), F32)
    vt_out = jnp.zeros((LANES, nrow), F32)
    for g in range(NSA_KV_HEADS):
        k_out = k_out + jnp.dot(hidden(0, g), w2k_ref[g], preferred_element_type=F32)
        vt_out = vt_out + _nt(w2vt_ref[g], hidden(1, g))
    k_ref[0] = k_out.astype(BF16)
    vt_ref[0] = vt_out.astype(BF16)


def _compress(chunks, w1, pe, w2k, w2vt):
    bsz, nrow, width = chunks[0].shape
    const = lambda a: pl.BlockSpec(a.shape, lambda b: (0,) * a.ndim)
    chunk_spec = pl.BlockSpec((1, nrow, width), lambda b: (b, 0, 0))
    return pl.pallas_call(
        _compress_kernel,
        grid=(bsz,),
        in_specs=[chunk_spec] * 4 + [const(w1), const(pe), const(w2k), const(w2vt)],
        out_specs=[pl.BlockSpec((1, nrow, LANES), lambda b: (b, 0, 0)),
                   pl.BlockSpec((1, LANES, nrow), lambda b: (b, 0, 0))],
        out_shape=[jax.ShapeDtypeStruct((bsz, nrow, LANES), BF16),
                   jax.ShapeDtypeStruct((bsz, LANES, nrow), BF16)],
        compiler_params=pltpu.CompilerParams(dimension_semantics=("arbitrary",),
                                             vmem_limit_bytes=VMEM_LIMIT),
        name="compress",
    )(*chunks, w1, pe, w2k, w2vt)


NSA_ROWS = NSA_GROUP * SEL_BLOCK
NSA_PAIR = 2
NSA_LANES = NSA_KV_HEADS * NSA_PAIR * NSA_ROWS
SEL_CHUNK = VT_CHUNK
WIN_KEYS = WINDOW + VW_CHUNK
ONES_ROWS = 16


def _with_ones(vt):
    return jnp.concatenate([vt, jnp.ones((ONES_ROWS, vt.shape[1]), vt.dtype)], axis=0)


def _nsa_kernel(qr_ref, qc_ref, ks_ref, oh_ref, vs_ref, kw_ref, vw_ref, kc_ref, vct_ref, gate_ref,
                ovt_ref, dtab_ref, ctab_ref, o_ref, m_ref, acc_ref, s_ref):
    ip = pl.program_id(1)
    glanes = NSA_PAIR * NSA_ROWS
    assert VW_CHUNK == NSA_PAIR * SEL_BLOCK and SEL_CHUNK % (NSA_PAIR * SEL_BLOCK) == 0

    def stack_heads(ref):
        return jnp.concatenate(
            [ref[0, blk * SEL_BLOCK:(blk + 1) * SEL_BLOCK, (g * NSA_GROUP + r) * LANES:(g * NSA_GROUP + r + 1) * LANES]
             for g in range(NSA_KV_HEADS) for blk in range(NSA_PAIR) for r in range(NSA_GROUP)], axis=0)

    def pv(vts, p):
        return jnp.concatenate(
            [jnp.dot(_with_ones(vts[g * HEAD_DIM:(g + 1) * HEAD_DIM]), p[:, g * glanes:(g + 1) * glanes],
                     preferred_element_type=F32) for g in range(NSA_KV_HEADS)], axis=1)

    def block_of(lane_iota, width):
        return NSA_PAIR * ip + ((lane_iota // width) % NSA_PAIR)

    lane128 = lax.broadcasted_iota(jnp.int32, (SEL_BLOCK, LANES), 1)
    tok128 = lax.broadcasted_iota(jnp.int32, (SEL_BLOCK, LANES), 0)
    lower = lane128 < HEAD_DIM
    gates = gate_ref[0, 0]
    i_row = block_of(lax.broadcasted_iota(jnp.int32, (1, NSA_LANES), 1), NSA_ROWS)
    t0_row = i_row * SEL_BLOCK

    qc = stack_heads(qc_ref)
    ctab = ctab_ref[...]
    any_valid = jnp.where(ctab[0:1] <= t0_row, 1.0, 0.0)
    s = jnp.where(ctab <= t0_row, _nt(kc_ref[0], qc), NEG)
    e = jnp.exp(s - jnp.max(s, axis=0, keepdims=True))
    e_hi, e_lo = _split_bf16(e)
    acc = pv(vct_ref[0], e_hi)
    inv = any_valid / acc[HEAD_DIM:HEAD_DIM + 1]
    out = (gates[0:1] * inv) * acc[0:HEAD_DIM]
    imp4 = (jnp.dot(ovt_ref[...], e_hi, preferred_element_type=F32)
            + jnp.dot(ovt_ref[...], e_lo, preferred_element_type=F32)) * inv
    imps = {}
    for g in range(NSA_KV_HEADS):
        for blk in range(NSA_PAIR):
            c0 = (g * NSA_PAIR + blk) * NSA_ROWS
            two = imp4[:, c0:c0 + LANES] + imp4[:, c0 + LANES:c0 + 2 * LANES]
            imps[g, blk] = two + pltpu.roll(two, HEAD_DIM, 1)
    imp_t = jnp.concatenate([jnp.where(lower, imps[0, blk], imps[1, blk]) for blk in range(NSA_PAIR)],
                            axis=1)

    qr = stack_heads(qr_ref)
    nwin = WIN_KEYS // VW_CHUNK
    wb0 = jnp.maximum(ip - WINDOW // VW_CHUNK, 0)
    wrows = pl.ds(pl.multiple_of(wb0 * VW_CHUNK, VW_CHUNK), WIN_KEYS)
    dist = (t0_row - wb0 * VW_CHUNK) - dtab_ref[...]
    s = jnp.where(lax.bitcast_convert_type(dist, jnp.uint32) < WINDOW, _nt(kw_ref[0, wrows, :], qr), NEG)
    e = jnp.exp(s - jnp.max(s, axis=0, keepdims=True)).astype(BF16)
    acc = pv(jnp.concatenate([vw_ref[wb0 + n] for n in range(nwin)], axis=1), e)
    out = out + (gates[2:3] * (1.0 / acc[HEAD_DIM:HEAD_DIM + 1])) * acc[0:HEAD_DIM]

    nb_pad, nsel = imp_t.shape
    jj = lax.broadcasted_iota(jnp.int32, (nb_pad, nsel), 0)
    i_sel = block_of(lax.broadcasted_iota(jnp.int32, (1, nsel), 1), LANES)
    forced = (jj == 0) | (jj == i_sel) | (jj == i_sel - 1)
    val = jnp.where(jj > i_sel, -jnp.inf, jnp.where(forced, FORCE_SCORE, imp_t))
    sub = 8
    blocks = [val[v * sub:(v + 1) * sub, :] for v in range(nb_pad // sub)]
    ranks = [jnp.zeros((sub, nsel), F32) for _ in blocks]
    jloc = lax.broadcasted_iota(jnp.int32, (sub, nsel), 0)
    for k in range(nb_pad):
        rowk = jnp.broadcast_to(val[k:k + 1, :], (sub, nsel))
        for v in range(nb_pad // sub):
            if v * sub > k:
                beats = jnp.where(rowk >= blocks[v], 1.0, 0.0)
            elif (v + 1) * sub <= k:
                beats = jnp.where(rowk > blocks[v], 1.0, 0.0)
            else:
                beats = jnp.where(jloc > k - v * sub,
                                  jnp.where(rowk >= blocks[v], 1.0, 0.0),
                                  jnp.where(rowk > blocks[v], 1.0, 0.0))
            ranks[v] = ranks[v] + beats
    rank = jnp.concatenate(ranks, axis=0)
    sel_bias_t = jnp.where(rank < float(SEL_TOPK), jnp.where(jj <= i_sel, 0.0, NEG), NEG).astype(BF16)
    xg = jnp.concatenate([sel_bias_t, jnp.zeros_like(sel_bias_t)], axis=0)
    lane_sel = lax.broadcasted_iota(jnp.int32, (SEL_BLOCK, nsel), 1)
    tok_sel = lax.broadcasted_iota(jnp.int32, (SEL_BLOCK, nsel), 0)
    bias_rows = []
    for g in range(NSA_KV_HEADS):
        for blk in range(NSA_PAIR):
            pick = jnp.where(lane_sel == tok_sel + blk * LANES + g * SEL_BLOCK, 1.0, 0.0).astype(BF16)
            bias_rows += [_nt(pick, xg).astype(BF16)] * NSA_GROUP
    q_aug = jnp.concatenate([qr, jnp.concatenate(bias_rows, axis=0)], axis=1)

    m_ref[...] = jnp.full(m_ref.shape, NEG, F32)
    acc_ref[...] = jnp.zeros(acc_ref.shape, F32)
    per_chunk = SEL_CHUNK // SEL_BLOCK
    nfull = (NSA_PAIR * ip) // per_chunk

    def scores(c, slot):
        rows = pl.ds(pl.multiple_of(c * SEL_CHUNK, SEL_CHUNK), SEL_CHUNK)
        k_aug = jnp.concatenate([ks_ref[0, rows, :], oh_ref[rows, :]], axis=1)
        s_ref[slot] = _nt(k_aug, q_aug)

    def absorb(c, slot, diag):
        s = s_ref[slot]
        if diag:
            s = jnp.where(dtab_ref[0:SEL_CHUNK, :] > (i_row % per_chunk) * SEL_BLOCK, NEG, s)
        m_old = m_ref[...]
        m_new = jnp.maximum(m_old, jnp.max(s, axis=0, keepdims=True))
        p = jnp.exp(s - m_new).astype(BF16)
        acc_ref[...] = jnp.exp(m_old - m_new) * acc_ref[...] + pv(vs_ref[c], p)
        m_ref[...] = m_new

    scores(0, 0)

    def pair(jp, carry):
        c = 2 * jp
        scores(c + 1, 1)
        absorb(c, 0, False)
        scores(c + 2, 0)
        absorb(c + 1, 1, False)
        return carry

    lax.fori_loop(0, nfull // 2, pair, 0)

    @pl.when(nfull % 2 == 1)
    def _():
        scores(nfull, 1)
        absorb(nfull - 1, 0, False)
        absorb(nfull, 1, True)

    @pl.when(nfull % 2 == 0)
    def _():
        absorb(nfull, 0, True)

    acc = acc_ref[...]
    out = (out + (gates[1:2] * (1.0 / acc[HEAD_DIM:HEAD_DIM + 1])) * acc[0:HEAD_DIM]).astype(BF16)

    fold = jnp.where((lane128 & (SEL_BLOCK - 1)) == tok128, 1.0, 0.0).astype(BF16)
    zero = jnp.zeros((HEAD_DIM, LANES), BF16)
    for blk in range(NSA_PAIR):
        for k in range(NSA_HEADS // 2):
            g, half = divmod(k, NSA_GROUP // 2)
            c0 = (g * NSA_PAIR + blk) * NSA_ROWS + half * LANES
            x = out[:, c0:c0 + LANES]
            y = jnp.concatenate([jnp.where(lower, x, zero), jnp.where(lower, zero, x)], axis=0)
            o_ref[0, blk * SEL_BLOCK:(blk + 1) * SEL_BLOCK, k * LANES:(k + 1) * LANES] = _nt(fold, y).astype(BF16)


def _nsa(proj3, gates_t, onehot, vt512, vw128, kcmp, vct, ovt, dtab, ctab):
    bsz, seq, _ = proj3.shape
    nstep = seq // (NSA_PAIR * SEL_BLOCK)
    tq = NSA_PAIR * SEL_BLOCK
    qw = NSA_HEADS * LANES
    const = lambda a: pl.BlockSpec(a.shape, lambda b, i: (0,) * a.ndim)
    per_b = lambda a: pl.BlockSpec((1,) + a.shape[1:], lambda b, i: (b,) + (0,) * (a.ndim - 1))
    return pl.pallas_call(
        _nsa_kernel,
        grid=(bsz, nstep),
        in_specs=[pl.BlockSpec((1, tq, qw), lambda b, i: (b, i, COL_QR // qw)),
                  pl.BlockSpec((1, tq, qw), lambda b, i: (b, i, COL_QC // qw)),
                  pl.BlockSpec((1, seq, LANES), lambda b, i: (b, 0, COL_KS // LANES)),
                  const(onehot),
                  pl.BlockSpec((seq // VT_CHUNK, LANES, VT_CHUNK), lambda b, i: (b, ROW_VS // LANES, 0)),
                  pl.BlockSpec((1, seq, LANES), lambda b, i: (b, 0, COL_KW // LANES)),
                  pl.BlockSpec((seq // VW_CHUNK, LANES, VW_CHUNK), lambda b, i: (b, 0, 0)),
                  per_b(kcmp), per_b(vct),
                  pl.BlockSpec((1, 1) + gates_t.shape[2:], lambda b, i: (b, i, 0, 0)),
                  const(ovt), const(dtab), const(ctab)],
        out_specs=pl.BlockSpec((1, tq, NSA_HEADS * HEAD_DIM), lambda b, i: (b, i, 0)),
        out_shape=jax.ShapeDtypeStruct((bsz, seq, NSA_HEADS * HEAD_DIM), BF16),
        scratch_shapes=[pltpu.VMEM((1, NSA_LANES), F32), pltpu.VMEM((HEAD_DIM + ONES_ROWS, NSA_LANES), F32),
                        pltpu.VMEM((2, SEL_CHUNK, NSA_LANES), F32)],
        compiler_params=pltpu.CompilerParams(dimension_semantics=("arbitrary", "arbitrary"),
                                             vmem_limit_bytes=VMEM_LIMIT),
        name="nsa",
    )(proj3, proj3, proj3, onehot, vt512, proj3, vw128, kcmp, vct, gates_t, ovt, dtab, ctab)


DIFF_TQ = VT_CHUNK


def _diff_kernel(q_ref, k_ref, vt_ref, lam_ref, ng_ref, o_ref, m_ref, acc_ref, s_ref, *, lambda_init):
    i = pl.program_id(2)
    tq = DIFF_TQ
    nv = 2 * HEAD_DIM
    q = q_ref[0]
    lane = lax.broadcasted_iota(jnp.int32, (tq, LANES), 1)
    zero = jnp.zeros_like(q)
    q_both = jnp.concatenate([jnp.where(lane < HEAD_DIM, q, zero), jnp.where(lane >= HEAD_DIM, q, zero)], axis=0)
    m_ref[...] = jnp.full(m_ref.shape, NEG, F32)
    acc_ref[...] = jnp.zeros(acc_ref.shape, F32)

    def scores(c, slot):
        k = k_ref[0, pl.ds(pl.multiple_of(c * tq, tq), tq), :]
        s_ref[slot] = _nt(k, q_both)

    def absorb(c, slot, causal):
        s = s_ref[slot]
        if causal:
            kpos = lax.broadcasted_iota(jnp.int32, (tq, 2 * tq), 0)
            qpos = lax.broadcasted_iota(jnp.int32, (tq, 2 * tq), 1) & (tq - 1)
            s = jnp.where(kpos <= qpos, s, NEG)
        m_old = m_ref[...]
        m_new = jnp.maximum(m_old, jnp.max(s, axis=0, keepdims=True))
        p = jnp.exp(s - m_new).astype(BF16)
        acc_ref[...] = jnp.exp(m_old - m_new) * acc_ref[...] + jnp.dot(_with_ones(vt_ref[c]), p,
                                                                       preferred_element_type=F32)
        m_ref[...] = m_new

    scores(0, 0)

    def pair(jp, carry):
        c = 2 * jp
        scores(c + 1, 1)
        absorb(c, 0, False)
        scores(c + 2, 0)
        absorb(c + 1, 1, False)
        return carry

    lax.fori_loop(0, i // 2, pair, 0)

    @pl.when(i % 2 == 1)
    def _():
        scores(i, 1)
        absorb(i - 1, 0, False)
        absorb(i, 1, True)

    @pl.when(i % 2 == 0)
    def _():
        absorb(i, 0, True)

    lv = lam_ref[...]
    lam = (jnp.exp(jnp.sum(lv[0:1] * lv[1:2], axis=1, keepdims=True))
           - jnp.exp(jnp.sum(lv[2:3] * lv[3:4], axis=1, keepdims=True)) + lambda_init)
    acc = acc_ref[...]
    on = acc[0:nv] * (1.0 / acc[nv:nv + 1])
    o = on[:, 0:tq] - lam * on[:, tq:2 * tq]
    o = o * lax.rsqrt(jnp.mean(o * o, axis=0, keepdims=True) + EPS) * ng_ref[...] * (1.0 - lambda_init)
    o_ref[0] = o.T.astype(BF16)


def _diff(proj3, vt512, lam_vec, norm_g, lambda_init):
    bsz, seq, _ = proj3.shape
    tq = DIFF_TQ
    kern = functools.partial(_diff_kernel, lambda_init=lambda_init)
    ng = jnp.broadcast_to(norm_g.reshape(LANES, 1), (LANES, tq))
    return pl.pallas_call(
        kern,
        grid=(bsz, DIFF_HEADS, seq // tq),
        in_specs=[pl.BlockSpec((1, tq, LANES), lambda b, h, i: (b, i, COL_QD // LANES + h)),
                  pl.BlockSpec((1, seq, LANES), lambda b, h, i: (b, 0, COL_KD // LANES + h)),
                  pl.BlockSpec((seq // VT_CHUNK, LANES, VT_CHUNK), lambda b, h, i: (b, ROW_VD // LANES + h, 0)),
                  pl.BlockSpec(lam_vec.shape, lambda b, h, i: (0, 0)),
                  pl.BlockSpec((LANES, tq), lambda b, h, i: (0, 0))],
        out_specs=pl.BlockSpec((1, tq, LANES), lambda b, h, i: (b, i, h)),
        out_shape=jax.ShapeDtypeStruct((bsz, seq, DIFF_HEADS * LANES), BF16),
        scratch_shapes=[pltpu.VMEM((1, 2 * tq), F32), pltpu.VMEM((2 * HEAD_DIM + ONES_ROWS, 2 * tq), F32),
                        pltpu.VMEM((2, tq, 2 * tq), F32)],
        compiler_params=pltpu.CompilerParams(dimension_semantics=("arbitrary", "arbitrary", "arbitrary"),
                                             vmem_limit_bytes=VMEM_LIMIT),
        name="diff",
    )(proj3, proj3, vt512, lam_vec, ng)


def _merge_kernel(x_ref, ada_ref, on_ref, od_ref, wg_ref, wbn_ref, wbd_ref, wo_ref, g_ref, b_ref, o_ref, *, alpha):
    d = x_ref.shape[1]
    x = x_ref[...]
    sh = ada_ref[0, 0:1, :]
    sc = ada_ref[0, 1:2, :]
    gate = ada_ref[0, 2:3, :]
    u = (x * (1.0 + sc) + sh).astype(BF16)
    gm = jax.nn.sigmoid(jnp.dot(u, wg_ref[...], preferred_element_type=F32))
    y_nsa = jnp.dot(on_ref[...], wbn_ref[...], preferred_element_type=F32)
    y_diff = jnp.dot(od_ref[...], wbd_ref[...], preferred_element_type=F32)
    mixed = (gm[:, 0:d] * y_nsa + gm[:, d:2 * d] * y_diff).astype(BF16)
    mix = jnp.dot(mixed, wo_ref[...], preferred_element_type=F32)
    o_ref[...] = _layer_norm(alpha * x + (1.0 + gate) * mix, g_ref[...], b_ref[...])


def _merge(x2, ada3, o_nsa, o_diff, w_gm, w_bn, w_bd, w_o, ln_g, ln_b, seq, alpha):
    m, d = x2.shape
    tm = min(512, seq)
    per_seq = seq // tm
    full = lambda a: pl.BlockSpec(a.shape, lambda i: (0,) * a.ndim)
    kern = functools.partial(_merge_kernel, alpha=alpha)
    return pl.pallas_call(
        kern,
        grid=(m // tm,),
        in_specs=[pl.BlockSpec((tm, d), lambda i: (i, 0)),
                  pl.BlockSpec((1, 6, d), lambda i: (i // per_seq, 0, 0)),
                  pl.BlockSpec((tm, o_nsa.shape[1]), lambda i: (i, 0)),
                  pl.BlockSpec((tm, o_diff.shape[1]), lambda i: (i, 0)),
                  full(w_gm), full(w_bn), full(w_bd), full(w_o),
                  pl.BlockSpec((1, d), lambda i: (0, 0)),
                  pl.BlockSpec((1, d), lambda i: (0, 0))],
        out_specs=pl.BlockSpec((tm, d), lambda i: (i, 0)),
        out_shape=jax.ShapeDtypeStruct((m, d), F32),
        compiler_params=pltpu.CompilerParams(dimension_semantics=("arbitrary",),
                                             vmem_limit_bytes=VMEM_LIMIT),
        name="merge",
    )(x2, ada3, o_nsa, o_diff, w_gm, w_bn, w_bd, w_o, ln_g.reshape(1, d), ln_b.reshape(1, d))


def _ffn_kernel(x_ref, ada_ref, wab_ref, wo_ref, g_ref, b_ref, o_ref, u_ref, acc_ref, *, alpha):
    f = pl.program_id(1)

    @pl.when(f == 0)
    def _():
        sh = ada_ref[0, 3:4, :]
        sc = ada_ref[0, 4:5, :]
        u_ref[...] = (x_ref[...] * (1.0 + sc) + sh).astype(BF16)
        acc_ref[...] = jnp.zeros(acc_ref.shape, F32)

    u = u_ref[...]
    ab = jnp.dot(u, wab_ref[...], preferred_element_type=F32)
    tf = wo_ref.shape[0]
    a = ab[:, 0:tf]
    b = ab[:, tf:2 * tf]
    h = ((a * jax.nn.sigmoid(a)) * b).astype(BF16)
    acc_ref[...] += jnp.dot(h, wo_ref[...], preferred_element_type=F32)

    @pl.when(f == pl.num_programs(1) - 1)
    def _():
        gate = ada_ref[0, 5:6, :]
        o_ref[...] = _layer_norm(alpha * x_ref[...] + (1.0 + gate) * acc_ref[...], g_ref[...], b_ref[...])


def _ffn(x2, ada3, w_in, w_out, ln_g, ln_b, seq, alpha):
    m, d = x2.shape
    dff = w_out.shape[0]
    tm = min(512, seq)
    tf = dff // 2
    nf = dff // tf
    per_seq = seq // tm
    w_ab = jnp.concatenate([w_in[:, half * dff + f * tf:half * dff + (f + 1) * tf]
                            for f in range(nf) for half in range(2)], axis=1)
    kern = functools.partial(_ffn_kernel, alpha=alpha)
    return pl.pallas_call(
        kern,
        grid=(m // tm, nf),
        in_specs=[pl.BlockSpec((tm, d), lambda i, f: (i, 0)),
                  pl.BlockSpec((1, 6, d), lambda i, f: (i // per_seq, 0, 0)),
                  pl.BlockSpec((d, 2 * tf), lambda i, f: (0, f)),
                  pl.BlockSpec((tf, d), lambda i, f: (f, 0)),
                  pl.BlockSpec((1, d), lambda i, f: (0, 0)),
                  pl.BlockSpec((1, d), lambda i, f: (0, 0))],
        out_specs=pl.BlockSpec((tm, d), lambda i, f: (i, 0)),
        out_shape=jax.ShapeDtypeStruct((m, d), F32),
        scratch_shapes=[pltpu.VMEM((tm, d), BF16), pltpu.VMEM((tm, d), F32)],
        compiler_params=pltpu.CompilerParams(dimension_semantics=("arbitrary", "arbitrary"),
                                             vmem_limit_bytes=VMEM_LIMIT),
        name="ffn",
    )(x2, ada3, w_ab, w_out, ln_g.reshape(1, d), ln_b.reshape(1, d))


def _rope_tables(seq):
    half = ROPE_DIMS // 2
    inv_freq = ROPE_THETA ** (-jnp.arange(half, dtype=F32) * 2.0 / ROPE_DIMS)
    ang = jnp.arange(seq, dtype=F32)[:, None] * inv_freq[None, :]
    cos, sin = jnp.cos(ang), jnp.sin(ang)
    ones = jnp.ones((seq, HEAD_DIM - ROPE_DIMS), F32)
    zeros = jnp.zeros((seq, HEAD_DIM - ROPE_DIMS), F32)
    z8 = jnp.zeros((seq, half), F32)
    cos_h = jnp.concatenate([cos, cos, ones], axis=1)
    sa_h = jnp.concatenate([-sin, z8, zeros], axis=1)
    sb_h = jnp.concatenate([z8, sin, zeros], axis=1)
    rep = lambda t: jnp.concatenate([t] * (LANES // HEAD_DIM), axis=1)
    return rep(cos_h), rep(sa_h), rep(sb_h)


def _mask_tables(seq):
    nb = seq // SEL_BLOCK
    nc = (seq - CMP_BLOCK) // CMP_STRIDE + 1
    ncp = seq // CMP_STRIDE
    cs = np.arange(nc) * CMP_STRIDE
    bs = np.arange(nb) * SEL_BLOCK
    ov = np.minimum(cs[:, None] + CMP_BLOCK, bs[None, :] + SEL_BLOCK) - np.maximum(cs[:, None], bs[None, :])
    ov = np.clip(ov, 0, None) / CMP_BLOCK
    ovt = np.zeros((SEL_BLOCK, ncp), np.float32)
    ovt[:nb, :nc] = ov.T
    tl = np.arange(NSA_LANES) % SEL_BLOCK
    dtab = (np.arange(WIN_KEYS)[:, None] - tl[None, :]).astype(np.int32)
    ctab = (np.arange(ncp)[:, None] * CMP_STRIDE + (CMP_BLOCK - 1) - tl[None, :]).astype(np.int32)
    return jnp.asarray(ovt, BF16), jnp.asarray(dtab), jnp.asarray(ctab)


def _proj_weight(w_in_l):
    d = w_in_l.shape[0]
    sizes = (512, 128, 128, 128, 128, 128, 128, 24, 512, 512, 512, 2 * d)
    offs = np.cumsum((0,) + sizes)
    q_n, kc, vc, ks, vs, kw, vw, g_n, q_d, k_d, v_d, g_m = [w_in_l[:, offs[k]:offs[k + 1]] for k in range(12)]
    qh = q_n.reshape(d, NSA_HEADS, HEAD_DIM)
    z = jnp.zeros_like(qh)
    first = jnp.arange(NSA_HEADS)[None, :, None] < NSA_GROUP
    q_wide = jnp.concatenate([jnp.where(first, qh, z), jnp.where(first, z, qh)], axis=-1).reshape(d, NSA_HEADS * LANES)
    zeros = lambda n: jnp.zeros((d, n), w_in_l.dtype)
    half_slabs = [jnp.concatenate([t[:, g * HEAD_DIM:(g + 1) * HEAD_DIM], zeros(LANES - HEAD_DIM)], axis=1)
                  for t in (kc, vc) for g in range(NSA_KV_HEADS)]
    w = jnp.concatenate([q_wide, q_wide, q_d, k_d, ks, kw, g_n, zeros(COL_KV - COL_GN - g_n.shape[1])]
                        + half_slabs, axis=1)
    assert w.shape[1] == PROJ_COLS
    wtv = jnp.concatenate([v_d, vs, vw], axis=1).T
    scale = np.ones((1, PROJ_COLS), np.float32)
    scale[:, COL_QR:COL_QD + 512] = QK_SCALE
    return w.astype(BF16), wtv.astype(BF16), jnp.asarray(scale), g_m.astype(BF16)


def kernel(x, c, w_ada, b_ada, w_in, cmp_pe_k, cmp_w1_k, cmp_w2_k, cmp_pe_v, cmp_w1_v, cmp_w2_v, diff_lambda, diff_norm_g, w_branch_nsa, w_branch_diff, w_out, ln1_g, ln1_b, w_ffn_in, w_ffn_out, ln2_g, ln2_b):
    bsz, seq, d = x.shape
    depth = w_ada.shape[0]
    assert seq % 1024 == 0, seq
    assert seq // SEL_BLOCK <= SEL_BLOCK
    alpha = (2 * depth) ** 0.25
    ncp = seq // CMP_STRIDE
    nblk = seq // SEL_BLOCK
    cos_t, sa_t, sb_t = _rope_tables(seq)
    ovt, dtab, ctab = _mask_tables(seq)
    onehot = (jnp.arange(seq)[:, None] // SEL_BLOCK == jnp.arange(LANES)[None, :]).astype(BF16)

    x2 = x.reshape(bsz * seq, d)
    for l in range(depth):
        lambda_init = 0.8 - 0.6 * math.exp(-0.3 * l)
        ada3 = _ada(c, w_ada[l], b_ada[l]).reshape(bsz, 6, d)
        w_proj, wtv, colscale, w_gm = _proj_weight(w_in[l])
        proj, gates, vt512, vw128, *kv_raw = _proj(x2, ada3, w_proj, wtv, colscale, cos_t, sa_t, sb_t, seq)
        proj3 = proj.reshape(bsz, seq, PROJ_COLS)
        chunks = [t.reshape(bsz, ncp, CMP_STRIDE * HEAD_DIM) for t in kv_raw]
        w1 = jnp.stack([cmp_w1_k[l], cmp_w1_v[l]]).astype(BF16)
        pe = jnp.stack([cmp_pe_k[l], cmp_pe_v[l]]).reshape(2, 1, CMP_BLOCK * HEAD_DIM)
        pe = jnp.broadcast_to(pe, (2, 8, CMP_BLOCK * HEAD_DIM)).astype(BF16)
        w2k = cmp_w2_k[l].astype(BF16)
        w2v = cmp_w2_v[l].astype(BF16).T
        w2k = jnp.stack([jnp.concatenate([w2k, jnp.zeros_like(w2k)], axis=1),
                         jnp.concatenate([jnp.zeros_like(w2k), w2k], axis=1)])
        w2vt = jnp.stack([jnp.concatenate([w2v, jnp.zeros_like(w2v)], axis=0),
                          jnp.concatenate([jnp.zeros_like(w2v), w2v], axis=0)])
        kcmp, vct = _compress(chunks, w1, pe, w2k, w2vt)

        nstep = nblk // NSA_PAIR
        gates_t = gates.reshape(bsz, nstep, NSA_PAIR, SEL_BLOCK, LANES)[..., :NSA_HEADS * 3]
        gates_t = gates_t.reshape(bsz, nstep, NSA_PAIR, SEL_BLOCK, NSA_KV_HEADS, NSA_GROUP, 3)
        gates_t = gates_t.transpose(0, 1, 6, 4, 2, 5, 3).reshape(bsz, nstep, 3, NSA_LANES)
        gates_t = jnp.pad(gates_t, ((0, 0), (0, 0), (0, 8 - 3), (0, 0)))
        o_nsa = _nsa(proj3, gates_t, onehot, vt512, vw128, kcmp, vct, ovt, dtab, ctab)
        o_diff = _diff(proj3, vt512, diff_lambda[l], diff_norm_g[l], lambda_init)

        x2 = _merge(x2, ada3, o_nsa.reshape(bsz * seq, -1), o_diff.reshape(bsz * seq, -1), w_gm,
                    w_branch_nsa[l].astype(BF16), w_branch_diff[l].astype(BF16), w_out[l].astype(BF16),
                    ln1_g[l], ln1_b[l], seq, alpha)
        x2 = _ffn(x2, ada3, w_ffn_in[l].astype(BF16), w_ffn_out[l].astype(BF16), ln2_g[l], ln2_b[l], seq, alpha)
    return x2.reshape(bsz, seq, d)
```

```python
import functools
import math

import numpy as np
import jax
import jax.numpy as jnp
from jax import lax
from jax.experimental import pallas as pl
from jax.experimental.pallas import tpu as pltpu

F32 = jnp.float32
BF16 = jnp.bfloat16

HEAD_DIM = 64
NSA_HEADS = 8
NSA_KV_HEADS = 2
NSA_GROUP = NSA_HEADS // NSA_KV_HEADS
CMP_BLOCK = 32
CMP_STRIDE = 16
CMP_HIDDEN = 256
SEL_BLOCK = 64
SEL_TOPK = 16
WINDOW = 512
DIFF_HEADS = 4
ROPE_THETA = 500000.0
ROPE_DIMS = HEAD_DIM // 4
EPS = 1e-5
NEG = -1e30
FORCE_SCORE = 1e9
QK_SCALE = HEAD_DIM ** -0.5

LANES = 128
VMEM_LIMIT = 56 * 1024 * 1024

PROJ_TN = 1024
MXU_COLS = 256
COL_QR = 0
COL_QC = 1024
COL_QD = 2048
COL_KD = 2560
COL_KS = 3072
COL_KW = 3200
COL_GN = 3328
COL_KV = 3584
PROJ_COLS = 4096
ROPE_TILES = (0, 2)
PLAIN_TILES = (1,)
LAST_TILE = 3
ROW_VD = 0
ROW_VS = 512
ROW_VW = 640
VT_ROWS_ALL = 768
VT_CHUNK = 512
VW_CHUNK = 128


def _nt(a, b):
    return lax.dot_general(a, b, (((1,), (1,)), ((), ())), preferred_element_type=F32)


def _split_bf16(a):
    hi = a.astype(BF16)
    lo = (a - hi.astype(F32)).astype(BF16)
    return hi, lo


def _layer_norm(h, g, b):
    mu = jnp.mean(h, axis=-1, keepdims=True)
    d = h - mu
    var = jnp.mean(d * d, axis=-1, keepdims=True)
    return d * lax.rsqrt(var + EPS) * g + b


def _ada_kernel(c_ref, w_ref, b_ref, o_ref):
    c = c_ref[...]
    a = c * jax.nn.sigmoid(c)
    a_hi, a_lo = _split_bf16(a)
    w_hi, w_lo = _split_bf16(w_ref[...])
    acc = jnp.dot(a_hi, w_hi, preferred_element_type=F32)
    acc += jnp.dot(a_lo, w_hi, preferred_element_type=F32)
    acc += jnp.dot(a_hi, w_lo, preferred_element_type=F32)
    o_ref[...] = acc + b_ref[...]


def _ada(c, w, b):
    bsz, d = c.shape
    n = w.shape[1]
    tn = 1024
    return pl.pallas_call(
        _ada_kernel,
        grid=(n // tn,),
        in_specs=[pl.BlockSpec((bsz, d), lambda j: (0, 0)),
                  pl.BlockSpec((d, tn), lambda j: (0, j)),
                  pl.BlockSpec((1, tn), lambda j: (0, j))],
        out_specs=pl.BlockSpec((bsz, tn), lambda j: (0, j)),
        out_shape=jax.ShapeDtypeStruct((bsz, n), F32),
        compiler_params=pltpu.CompilerParams(dimension_semantics=("arbitrary",),
                                             vmem_limit_bytes=VMEM_LIMIT),
        name="ada",
    )(c, w, b.reshape(1, n))


def _proj_kernel(x_ref, ada_ref, w_ref, wtv_ref, cs_ref, cos_ref, sa_ref, sb_ref,
                 o_ref, g_ref, vt_ref, vw_ref, kc0_ref, kc1_ref, vc0_ref, vc1_ref, u_ref):
    j = pl.program_id(1)
    tm = u_ref.shape[0]
    tn = w_ref.shape[1]

    @pl.when(j == 0)
    def _():
        sh = ada_ref[0, 0:1, :]
        sc = ada_ref[0, 1:2, :]
        u_ref[...] = (x_ref[...] * (1.0 + sc) + sh).astype(BF16)
        vt = _nt(wtv_ref[...], u_ref[...]).astype(BF16)
        for cc in range(tm // VT_CHUNK):
            vt_ref[cc] = vt[ROW_VD:ROW_VW, cc * VT_CHUNK:(cc + 1) * VT_CHUNK]
        for cc in range(tm // VW_CHUNK):
            vw_ref[cc] = vt[ROW_VW:VT_ROWS_ALL, cc * VW_CHUNK:(cc + 1) * VW_CHUNK]

    def col_tile(ct):
        cols = slice(ct * MXU_COLS, (ct + 1) * MXU_COLS)
        return jnp.dot(u_ref[...], w_ref[:, cols], preferred_element_type=F32) * cs_ref[:, cols]

    def store_rope(a, ct):
        cosv, sav, sbv = cos_ref[...], sa_ref[...], sb_ref[...]
        for s in range(MXU_COLS // LANES):
            a_s = a[:, s * LANES:(s + 1) * LANES]
            r = a_s * cosv + pltpu.roll(a_s, LANES - ROPE_DIMS // 2, 1) * sav + pltpu.roll(a_s, ROPE_DIMS // 2, 1) * sbv
            o_ref[:, ct * MXU_COLS + s * LANES:ct * MXU_COLS + (s + 1) * LANES] = r.astype(BF16)

    def store_plain(a, ct):
        o_ref[:, ct * MXU_COLS:(ct + 1) * MXU_COLS] = a.astype(BF16)

    def any_of(tiles):
        return functools.reduce(jnp.logical_or, [j == t for t in tiles])

    @pl.when(any_of(ROPE_TILES))
    def _():
        for ct in range(tn // MXU_COLS):
            store_rope(col_tile(ct), ct)

    @pl.when(any_of(PLAIN_TILES))
    def _():
        for ct in range(tn // MXU_COLS):
            store_plain(col_tile(ct), ct)

    @pl.when(j == LAST_TILE)
    def _():
        gate_ct = (COL_GN - LAST_TILE * PROJ_TN) // MXU_COLS
        kv_ct = (COL_KV - LAST_TILE * PROJ_TN) // MXU_COLS
        kv_refs = ((kc0_ref, kc1_ref), (vc0_ref, vc1_ref))
        for ct in range(tn // MXU_COLS):
            a = col_tile(ct)
            if ct < kv_ct:
                if ct == gate_ct:
                    g_ref[...] = jax.nn.sigmoid(a[:, 0:LANES])
                store_rope(a, ct)
            else:
                for g, ref in enumerate(kv_refs[ct - kv_ct]):
                    ref[...] = a[:, g * LANES:g * LANES + HEAD_DIM].astype(BF16)
                store_plain(a, ct)


def _proj(x2, ada3, w, wtv, colscale, cos_t, sa_t, sb_t, seq):
    m, d = x2.shape
    tm = min(1024, seq)
    tn = PROJ_TN
    per_seq = seq // tm
    kv_spec = pl.BlockSpec((tm, HEAD_DIM), lambda i, j: (i, 0))
    kv_shape = jax.ShapeDtypeStruct((m, HEAD_DIM), BF16)
    return pl.pallas_call(
        _proj_kernel,
        grid=(m // tm, w.shape[1] // tn),
        in_specs=[pl.BlockSpec((tm, d), lambda i, j: (i, 0)),
                  pl.BlockSpec((1, 6, d), lambda i, j: (i // per_seq, 0, 0)),
                  pl.BlockSpec((d, tn), lambda i, j: (0, j)),
                  pl.BlockSpec(wtv.shape, lambda i, j: (0, 0)),
                  pl.BlockSpec((1, tn), lambda i, j: (0, j)),
                  pl.BlockSpec((tm, LANES), lambda i, j: (i % per_seq, 0)),
                  pl.BlockSpec((tm, LANES), lambda i, j: (i % per_seq, 0)),
                  pl.BlockSpec((tm, LANES), lambda i, j: (i % per_seq, 0))],
        out_specs=[pl.BlockSpec((tm, tn), lambda i, j: (i, j)),
                   pl.BlockSpec((tm, LANES), lambda i, j: (i, 0)),
                   pl.BlockSpec((tm // VT_CHUNK, ROW_VW, VT_CHUNK), lambda i, j: (i, 0, 0)),
                   pl.BlockSpec((tm // VW_CHUNK, VT_ROWS_ALL - ROW_VW, VW_CHUNK), lambda i, j: (i, 0, 0)),
                   kv_spec, kv_spec, kv_spec, kv_spec],
        out_shape=[jax.ShapeDtypeStruct((m, PROJ_COLS), BF16),
                   jax.ShapeDtypeStruct((m, LANES), F32),
                   jax.ShapeDtypeStruct((m // VT_CHUNK, ROW_VW, VT_CHUNK), BF16),
                   jax.ShapeDtypeStruct((m // VW_CHUNK, VT_ROWS_ALL - ROW_VW, VW_CHUNK), BF16),
                   kv_shape, kv_shape, kv_shape, kv_shape],
        scratch_shapes=[pltpu.VMEM((tm, d), BF16)],
        compiler_params=pltpu.CompilerParams(dimension_semantics=("arbitrary", "arbitrary"),
                                             vmem_limit_bytes=VMEM_LIMIT),
        name="proj",
    )(x2, ada3, w, wtv, colscale, cos_t, sa_t, sb_t)


def _gelu_tanh(x):
    return x * (0.5 * (1.0 + jnp.tanh(math.sqrt(2.0 / math.pi) * (x + 0.044715 * (x * x * x)))))


def _compress_kernel(kc0_ref, kc1_ref, vc0_ref, vc1_ref, w1_ref, pe_ref, w2k_ref, w2vt_ref, k_ref, vt_ref):
    xs = ((kc0_ref, kc1_ref), (vc0_ref, vc1_ref))
    nrow = kc0_ref.shape[1]
    half = w1_ref.shape[1] // 2

    def hidden(which, g):
        w1 = w1_ref[which]
        pe_row = jnp.dot(pe_ref[which], w1, preferred_element_type=F32)[0:1, :]
        x = xs[which][g][0]
        a = jnp.dot(x, w1[0:half], preferred_element_type=F32)
        b = jnp.dot(x, w1[half:2 * half], preferred_element_type=F32)
        return _gelu_tanh(a + pltpu.roll(b, nrow - 1, 0) + pe_row).astype(BF16)

    k_out = jnp.zeros((nrow, LANES), F32)
    vt_out = jnp.zeros((LANES, nrow), F32)
    for g in range(NSA_KV_HEADS):
        k_out = k_out + jnp.dot(hidden(0, g), w2k_ref[g], preferred_element_type=F32)
        vt_out = vt_out + _nt(w2vt_ref[g], hidden(1, g))
    k_ref[0] = k_out.astype(BF16)
    vt_ref[0] = vt_out.astype(BF16)


def _compress(chunks, w1, pe, w2k, w2vt):
    bsz, nrow, width = chunks[0].shape
    const = lambda a: pl.BlockSpec(a.shape, lambda b: (0,) * a.ndim)
    chunk_spec = pl.BlockSpec((1, nrow, width), lambda b: (b, 0, 0))
    return pl.pallas_call(
        _compress_kernel,
        grid=(bsz,),
        in_specs=[chunk_spec] * 4 + [const(w1), const(pe), const(w2k), const(w2vt)],
        out_specs=[pl.BlockSpec((1, nrow, LANES), lambda b: (b, 0, 0)),
                   pl.BlockSpec((1, LANES, nrow), lambda b: (b, 0, 0))],
        out_shape=[jax.ShapeDtypeStruct((bsz, nrow, LANES), BF16),
                   jax.ShapeDtypeStruct((bsz, LANES, nrow), BF16)],
        compiler_params=pltpu.CompilerParams(dimension_semantics=("arbitrary",),
                                             vmem_limit_bytes=VMEM_LIMIT),
        name="compress",
    )(*chunks, w1, pe, w2k, w2vt)


NSA_ROWS = NSA_GROUP * SEL_BLOCK
NSA_PAIR = 2
NSA_LANES = NSA_KV_HEADS * NSA_PAIR * NSA_ROWS
SEL_CHUNK = VT_CHUNK
WIN_KEYS = WINDOW + VW_CHUNK
ONES_ROWS = 16


def _with_ones(vt):
    return jnp.concatenate([vt, jnp.ones((ONES_ROWS, vt.shape[1]), vt.dtype)], axis=0)


def _nsa_kernel(qr_ref, qc_ref, ks_ref, oh_ref, vs_ref, kw_ref, vw_ref, kc_ref, vct_ref, gate_ref,
                ovt_ref, dtab_ref, ctab_ref, o_ref, m_ref, acc_ref, s_ref):
    ip = pl.program_id(1)
    glanes = NSA_PAIR * NSA_ROWS
    assert VW_CHUNK == NSA_PAIR * SEL_BLOCK and SEL_CHUNK % (NSA_PAIR * SEL_BLOCK) == 0

    def stack_heads(ref):
        return jnp.concatenate(
            [ref[0, blk * SEL_BLOCK:(blk + 1) * SEL_BLOCK, (g * NSA_GROUP + r) * LANES:(g * NSA_GROUP + r + 1) * LANES]
             for g in range(NSA_KV_HEADS) for blk in range(NSA_PAIR) for r in range(NSA_GROUP)], axis=0)

    def pv(vts, p):
        return jnp.concatenate(
            [jnp.dot(_with_ones(vts[g * HEAD_DIM:(g + 1) * HEAD_DIM]), p[:, g * glanes:(g + 1) * glanes],
                     preferred_element_type=F32) for g in range(NSA_KV_HEADS)], axis=1)

    def block_of(lane_iota, width):
        return NSA_PAIR * ip + ((lane_iota // width) % NSA_PAIR)

    lane128 = lax.broadcasted_iota(jnp.int32, (SEL_BLOCK, LANES), 1)
    tok128 = lax.broadcasted_iota(jnp.int32, (SEL_BLOCK, LANES), 0)
    lower = lane128 < HEAD_DIM
    gates = gate_ref[0, 0]
    i_row = block_of(lax.broadcasted_iota(jnp.int32, (1, NSA_LANES), 1), NSA_ROWS)
    t0_row = i_row * SEL_BLOCK

    qc = stack_heads(qc_ref)
    ctab = ctab_ref[...]
    any_valid = jnp.where(ctab[0:1] <= t0_row, 1.0, 0.0)
    s = jnp.where(ctab <= t0_row, _nt(kc_ref[0], qc), NEG)
    e = jnp.exp(s - jnp.max(s, axis=0, keepdims=True))
    e_hi, e_lo = _split_bf16(e)
    acc = pv(vct_ref[0], e_hi)
    inv = any_valid / acc[HEAD_DIM:HEAD_DIM + 1]
    out = (gates[0:1] * inv) * acc[0:HEAD_DIM]
    imp4 = (jnp.dot(ovt_ref[...], e_hi, preferred_element_type=F32)
            + jnp.dot(ovt_ref[...], e_lo, preferred_element_type=F32)) * inv
    imps = {}
    for g in range(NSA_KV_HEADS):
        for blk in range(NSA_PAIR):
            c0 = (g * NSA_PAIR + blk) * NSA_ROWS
            two = imp4[:, c0:c0 + LANES] + imp4[:, c0 + LANES:c0 + 2 * LANES]
            imps[g, blk] = two + pltpu.roll(two, HEAD_DIM, 1)
    imp_t = jnp.concatenate([jnp.where(lower, imps[0, blk], imps[1, blk]) for blk in range(NSA_PAIR)],
                            axis=1)

    qr = stack_heads(qr_ref)
    nwin = WIN_KEYS // VW_CHUNK
    wb0 = jnp.maximum(ip - WINDOW // VW_CHUNK, 0)
    wrows = pl.ds(pl.multiple_of(wb0 * VW_CHUNK, VW_CHUNK), WIN_KEYS)
    dist = (t0_row - wb0 * VW_CHUNK) - dtab_ref[...]
    s = jnp.where(lax.bitcast_convert_type(dist, jnp.uint32) < WINDOW, _nt(kw_ref[0, wrows, :], qr), NEG)
    e = jnp.exp(s - jnp.max(s, axis=0, keepdims=True)).astype(BF16)
    acc = pv(jnp.concatenate([vw_ref[wb0 + n] for n in range(nwin)], axis=1), e)
    out = out + (gates[2:3] * (1.0 / acc[HEAD_DIM:HEAD_DIM + 1])) * acc[0:HEAD_DIM]

    nb_pad, nsel = imp_t.shape
    jj = lax.broadcasted_iota(jnp.int32, (nb_pad, nsel), 0)
    i_sel = block_of(lax.broadcasted_iota(jnp.int32, (1, nsel), 1), LANES)
    forced = (jj == 0) | (jj == i_sel) | (jj == i_sel - 1)
    val = jnp.where(jj > i_sel, -jnp.inf, jnp.where(forced, FORCE_SCORE, imp_t))
    sub = 8
    blocks = [val[v * sub:(v + 1) * sub, :] for v in range(nb_pad // sub)]
    ranks = [jnp.zeros((sub, nsel), F32) for _ in blocks]
    jloc = lax.broadcasted_iota(jnp.int32, (sub, nsel), 0)
    for k in range(nb_pad):
        rowk = jnp.broadcast_to(val[k:k + 1, :], (sub, nsel))
        for v in range(nb_pad // sub):
            if v * sub > k:
                beats = jnp.where(rowk >= blocks[v], 1.0, 0.0)
            elif (v + 1) * sub <= k:
                beats = jnp.where(rowk > blocks[v], 1.0, 0.0)
            else:
                beats = jnp.where(jloc > k - v * sub,
                                  jnp.where(rowk >= blocks[v], 1.0, 0.0),
                                  jnp.where(rowk > blocks[v], 1.0, 0.0))
            ranks[v] = ranks[v] + beats
    rank = jnp.concatenate(ranks, axis=0)
    sel_bias_t = jnp.where(rank < float(SEL_TOPK), jnp.where(jj <= i_sel, 0.0, NEG), NEG).astype(BF16)
    xg = jnp.concatenate([sel_bias_t, jnp.zeros_like(sel_bias_t)], axis=0)
    lane_sel = lax.broadcasted_iota(jnp.int32, (SEL_BLOCK, nsel), 1)
    tok_sel = lax.broadcasted_iota(jnp.int32, (SEL_BLOCK, nsel), 0)
    bias_rows = []
    for g in range(NSA_KV_HEADS):
        for blk in range(NSA_PAIR):
            pick = jnp.where(lane_sel == tok_sel + blk * LANES + g * SEL_BLOCK, 1.0, 0.0).astype(BF16)
            bias_rows += [_nt(pick, xg).astype(BF16)] * NSA_GROUP
    q_aug = jnp.concatenate([qr, jnp.concatenate(bias_rows, axis=0)], axis=1)

    m_ref[...] = jnp.full(m_ref.shape, NEG, F32)
    acc_ref[...] = jnp.zeros(acc_ref.shape, F32)
    per_chunk = SEL_CHUNK // SEL_BLOCK
    nfull = (NSA_PAIR * ip) // per_chunk

    def scores(c, slot):
        rows = pl.ds(pl.multiple_of(c * SEL_CHUNK, SEL_CHUNK), SEL_CHUNK)
        k_aug = jnp.concatenate([ks_ref[0, rows, :], oh_ref[rows, :]], axis=1)
        s_ref[slot] = _nt(k_aug, q_aug)

    def absorb(c, slot, diag):
        s = s_ref[slot]
        if diag:
            s = jnp.where(dtab_ref[0:SEL_CHUNK, :] > (i_row % per_chunk) * SEL_BLOCK, NEG, s)
        m_old = m_ref[...]
        m_new = jnp.maximum(m_old, jnp.max(s, axis=0, keepdims=True))
        p = jnp.exp(s - m_new).astype(BF16)
        acc_ref[...] = jnp.exp(m_old - m_new) * acc_ref[...] + pv(vs_ref[c], p)
        m_ref[...] = m_new

    scores(0, 0)

    def pair(jp, carry):
        c = 2 * jp
        scores(c + 1, 1)
        absorb(c, 0, False)
        scores(c + 2, 0)
        absorb(c + 1, 1, False)
        return carry

    lax.fori_loop(0, nfull // 2, pair, 0)

    @pl.when(nfull % 2 == 1)
    def _():
        scores(nfull, 1)
        absorb(nfull - 1, 0, False)
        absorb(nfull, 1, True)

    @pl.when(nfull % 2 == 0)
    def _():
        absorb(nfull, 0, True)

    acc = acc_ref[...]
    out = (out + (gates[1:2] * (1.0 / acc[HEAD_DIM:HEAD_DIM + 1])) * acc[0:HEAD_DIM]).astype(BF16)

    fold = jnp.where((lane128 & (SEL_BLOCK - 1)) == tok128, 1.0, 0.0).astype(BF16)
    zero = jnp.zeros((HEAD_DIM, LANES), BF16)
    for blk in range(NSA_PAIR):
        for k in range(NSA_HEADS // 2):
            g, half = divmod(k, NSA_GROUP // 2)
            c0 = (g * NSA_PAIR + blk) * NSA_ROWS + half * LANES
            x = out[:, c0:c0 + LANES]
            y = jnp.concatenate([jnp.where(lower, x, zero), jnp.where(lower, zero, x)], axis=0)
            o_ref[0, blk * SEL_BLOCK:(blk + 1) * SEL_BLOCK, k * LANES:(k + 1) * LANES] = _nt(fold, y).astype(BF16)


def _nsa(proj3, gates_t, onehot, vt512, vw128, kcmp, vct, ovt, dtab, ctab):
    bsz, seq, _ = proj3.shape
    nstep = seq // (NSA_PAIR * SEL_BLOCK)
    tq = NSA_PAIR * SEL_BLOCK
    qw = NSA_HEADS * LANES
    const = lambda a: pl.BlockSpec(a.shape, lambda b, i: (0,) * a.ndim)
    per_b = lambda a: pl.BlockSpec((1,) + a.shape[1:], lambda b, i: (b,) + (0,) * (a.ndim - 1))
    return pl.pallas_call(
        _nsa_kernel,
        grid=(bsz, nstep),
        in_specs=[pl.BlockSpec((1, tq, qw), lambda b, i: (b, i, COL_QR // qw)),
                  pl.BlockSpec((1, tq, qw), lambda b, i: (b, i, COL_QC // qw)),
                  pl.BlockSpec((1, seq, LANES), lambda b, i: (b, 0, COL_KS // LANES)),
                  const(onehot),
                  pl.BlockSpec((seq // VT_CHUNK, LANES, VT_CHUNK), lambda b, i: (b, ROW_VS // LANES, 0)),
                  pl.BlockSpec((1, seq, LANES), lambda b, i: (b, 0, COL_KW // LANES)),
                  pl.BlockSpec((seq // VW_CHUNK, LANES, VW_CHUNK), lambda b, i: (b, 0, 0)),
                  per_b(kcmp), per_b(vct),
                  pl.BlockSpec((1, 1) + gates_t.shape[2:], lambda b, i: (b, i, 0, 0)),
                  const(ovt), const(dtab), const(ctab)],
        out_specs=pl.BlockSpec((1, tq, NSA_HEADS * HEAD_DIM), lambda b, i: (b, i, 0)),
        out_shape=jax.ShapeDtypeStruct((bsz, seq, NSA_HEADS * HEAD_DIM), BF16),
        scratch_shapes=[pltpu.VMEM((1, NSA_LANES), F32), pltpu.VMEM((HEAD_DIM + ONES_ROWS, NSA_LANES), F32),
                        pltpu.VMEM((2, SEL_CHUNK, NSA_LANES), F32)],
        compiler_params=pltpu.CompilerParams(dimension_semantics=("arbitrary", "arbitrary"),
                                             vmem_limit_bytes=VMEM_LIMIT),
        name="nsa",
    )(proj3, proj3, proj3, onehot, vt512, proj3, vw128, kcmp, vct, gates_t, ovt, dtab, ctab)


DIFF_TQ = VT_CHUNK
DIFF_PAIR = 2


def _diff_kernel(q_ref, k_ref, vt_ref, lam_ref, ng_ref, o_ref, m_ref, acc_ref, s_ref, *, lambda_init):
    i = pl.program_id(2)
    tq = DIFF_TQ
    nv = 2 * HEAD_DIM
    hl = 2 * tq
    lane = lax.broadcasted_iota(jnp.int32, (tq, LANES), 1)
    q_both = []
    for hh in range(DIFF_PAIR):
        q = q_ref[0, :, hh * LANES:(hh + 1) * LANES]
        zero = jnp.zeros_like(q)
        q_both.append(jnp.concatenate([jnp.where(lane < HEAD_DIM, q, zero), jnp.where(lane >= HEAD_DIM, q, zero)],
                                      axis=0))
    m_ref[...] = jnp.full(m_ref.shape, NEG, F32)
    acc_ref[...] = jnp.zeros(acc_ref.shape, F32)

    def scores(c, slot):
        rows = pl.ds(pl.multiple_of(c * tq, tq), tq)
        for hh in range(DIFF_PAIR):
            s_ref[slot, :, hh * hl:(hh + 1) * hl] = _nt(k_ref[0, rows, hh * LANES:(hh + 1) * LANES], q_both[hh])

    def absorb(c, slot, causal):
        s = s_ref[slot]
        if causal:
            kpos = lax.broadcasted_iota(jnp.int32, s.shape, 0)
            qpos = lax.broadcasted_iota(jnp.int32, s.shape, 1) & (tq - 1)
            s = jnp.where(kpos <= qpos, s, NEG)
        m_old = m_ref[...]
        m_new = jnp.maximum(m_old, jnp.max(s, axis=0, keepdims=True))
        p = jnp.exp(s - m_new).astype(BF16)
        vt = vt_ref[c]
        pv = jnp.concatenate([jnp.dot(_with_ones(vt[hh * nv:(hh + 1) * nv]), p[:, hh * hl:(hh + 1) * hl],
                                      preferred_element_type=F32) for hh in range(DIFF_PAIR)], axis=1)
        acc_ref[...] = jnp.exp(m_old - m_new) * acc_ref[...] + pv
        m_ref[...] = m_new

    scores(0, 0)

    def pair(jp, carry):
        c = 2 * jp
        scores(c + 1, 1)
        absorb(c, 0, False)
        scores(c + 2, 0)
        absorb(c + 1, 1, False)
        return carry

    lax.fori_loop(0, i // 2, pair, 0)

    @pl.when(i % 2 == 1)
    def _():
        scores(i, 1)
        absorb(i - 1, 0, False)
        absorb(i, 1, True)

    @pl.when(i % 2 == 0)
    def _():
        absorb(i, 0, True)

    lv = lam_ref[...]
    lam = (jnp.exp(jnp.sum(lv[0:1] * lv[1:2], axis=1, keepdims=True))
           - jnp.exp(jnp.sum(lv[2:3] * lv[3:4], axis=1, keepdims=True)) + lambda_init)
    acc = acc_ref[...]
    on = acc[0:nv] * (1.0 / acc[nv:nv + 1])
    for hh in range(DIFF_PAIR):
        o = on[:, hh * hl:hh * hl + tq] - lam * on[:, hh * hl + tq:(hh + 1) * hl]
        o = o * lax.rsqrt(jnp.mean(o * o, axis=0, keepdims=True) + EPS) * ng_ref[...] * (1.0 - lambda_init)
        o_ref[0, :, hh * LANES:(hh + 1) * LANES] = o.T.astype(BF16)


def _diff(proj3, vt512, lam_vec, norm_g, lambda_init):
    bsz, seq, _ = proj3.shape
    tq = DIFF_TQ
    pw = DIFF_PAIR * LANES
    lanes = DIFF_PAIR * 2 * tq
    kern = functools.partial(_diff_kernel, lambda_init=lambda_init)
    ng = jnp.broadcast_to(norm_g.reshape(LANES, 1), (LANES, tq))
    return pl.pallas_call(
        kern,
        grid=(bsz, DIFF_HEADS // DIFF_PAIR, seq // tq),
        in_specs=[pl.BlockSpec((1, tq, pw), lambda b, h, i: (b, i, COL_QD // pw + h)),
                  pl.BlockSpec((1, seq, pw), lambda b, h, i: (b, 0, COL_KD // pw + h)),
                  pl.BlockSpec((seq // VT_CHUNK, pw, VT_CHUNK), lambda b, h, i: (b, ROW_VD // pw + h, 0)),
                  pl.BlockSpec(lam_vec.shape, lambda b, h, i: (0, 0)),
                  pl.BlockSpec((LANES, tq), lambda b, h, i: (0, 0))],
        out_specs=pl.BlockSpec((1, tq, pw), lambda b, h, i: (b, i, h)),
        out_shape=jax.ShapeDtypeStruct((bsz, seq, DIFF_HEADS * LANES), BF16),
        scratch_shapes=[pltpu.VMEM((1, lanes), F32), pltpu.VMEM((2 * HEAD_DIM + ONES_ROWS, lanes), F32),
                        pltpu.VMEM((2, tq, lanes), F32)],
        compiler_params=pltpu.CompilerParams(dimension_semantics=("arbitrary", "arbitrary", "arbitrary"),
                                             vmem_limit_bytes=VMEM_LIMIT),
        name="diff",
    )(proj3, proj3, vt512, lam_vec, ng)


def _merge_kernel(x_ref, ada_ref, on_ref, od_ref, wg_ref, wbn_ref, wbd_ref, wo_ref, g_ref, b_ref, o_ref, *, alpha):
    d = x_ref.shape[1]
    x = x_ref[...]
    sh = ada_ref[0, 0:1, :]
    sc = ada_ref[0, 1:2, :]
    gate = ada_ref[0, 2:3, :]
    u = (x * (1.0 + sc) + sh).astype(BF16)
    gm = jax.nn.sigmoid(jnp.dot(u, wg_ref[...], preferred_element_type=F32))
    y_nsa = jnp.dot(on_ref[...], wbn_ref[...], preferred_element_type=F32)
    y_diff = jnp.dot(od_ref[...], wbd_ref[...], preferred_element_type=F32)
    mixed = (gm[:, 0:d] * y_nsa + gm[:, d:2 * d] * y_diff).astype(BF16)
    mix = jnp.dot(mixed, wo_ref[...], preferred_element_type=F32)
    o_ref[...] = _layer_norm(alpha * x + (1.0 + gate) * mix, g_ref[...], b_ref[...])


def _merge(x2, ada3, o_nsa, o_diff, w_gm, w_bn, w_bd, w_o, ln_g, ln_b, seq, alpha):
    m, d = x2.shape
    tm = min(512, seq)
    per_seq = seq // tm
    full = lambda a: pl.BlockSpec(a.shape, lambda i: (0,) * a.ndim)
    kern = functools.partial(_merge_kernel, alpha=alpha)
    return pl.pallas_call(
        kern,
        grid=(m // tm,),
        in_specs=[pl.BlockSpec((tm, d), lambda i: (i, 0)),
                  pl.BlockSpec((1, 6, d), lambda i: (i // per_seq, 0, 0)),
                  pl.BlockSpec((tm, o_nsa.shape[1]), lambda i: (i, 0)),
                  pl.BlockSpec((tm, o_diff.shape[1]), lambda i: (i, 0)),
                  full(w_gm), full(w_bn), full(w_bd), full(w_o),
                  pl.BlockSpec((1, d), lambda i: (0, 0)),
                  pl.BlockSpec((1, d), lambda i: (0, 0))],
        out_specs=pl.BlockSpec((tm, d), lambda i: (i, 0)),
        out_shape=jax.ShapeDtypeStruct((m, d), F32),
        compiler_params=pltpu.CompilerParams(dimension_semantics=("arbitrary",),
                                             vmem_limit_bytes=VMEM_LIMIT),
        name="merge",
    )(x2, ada3, o_nsa, o_diff, w_gm, w_bn, w_bd, w_o, ln_g.reshape(1, d), ln_b.reshape(1, d))


FFN_TILE = 256


def _ffn_kernel(x_ref, ada_ref, wab_ref, wo_ref, g_ref, b_ref, o_ref, *, alpha):
    x = x_ref[...]
    sh = ada_ref[0, 3:4, :]
    sc = ada_ref[0, 4:5, :]
    gate = ada_ref[0, 5:6, :]
    u = (x * (1.0 + sc) + sh).astype(BF16)
    acc = jnp.zeros(x.shape, F32)
    for c in range(wo_ref.shape[0] // FFN_TILE):
        ab = jnp.dot(u, wab_ref[:, 2 * c * FFN_TILE:2 * (c + 1) * FFN_TILE], preferred_element_type=F32)
        a = ab[:, 0:FFN_TILE]
        h = ((a * jax.nn.sigmoid(a)) * ab[:, FFN_TILE:2 * FFN_TILE]).astype(BF16)
        acc = acc + jnp.dot(h, wo_ref[c * FFN_TILE:(c + 1) * FFN_TILE, :], preferred_element_type=F32)
    o_ref[...] = _layer_norm(alpha * x + (1.0 + gate) * acc, g_ref[...], b_ref[...])


def _ffn(x2, ada3, w_in, w_out, ln_g, ln_b, seq, alpha):
    m, d = x2.shape
    dff = w_out.shape[0]
    tm = min(1024, seq)
    per_seq = seq // tm
    nt = dff // FFN_TILE
    w_ab = jnp.concatenate([w_in[:, half * dff + c * FFN_TILE:half * dff + (c + 1) * FFN_TILE]
                            for c in range(nt) for half in range(2)], axis=1)
    resident = lambda a: pl.BlockSpec(a.shape, lambda i: (0,) * a.ndim, pipeline_mode=pl.Buffered(1))
    kern = functools.partial(_ffn_kernel, alpha=alpha)
    return pl.pallas_call(
        kern,
        grid=(m // tm,),
        in_specs=[pl.BlockSpec((tm, d), lambda i: (i, 0)),
                  pl.BlockSpec((1, 6, d), lambda i: (i // per_seq, 0, 0)),
                  resident(w_ab), resident(w_out),
                  pl.BlockSpec((1, d), lambda i: (0, 0)),
                  pl.BlockSpec((1, d), lambda i: (0, 0))],
        out_specs=pl.BlockSpec((tm, d), lambda i: (i, 0)),
        out_shape=jax.ShapeDtypeStruct((m, d), F32),
        compiler_params=pltpu.CompilerParams(dimension_semantics=("arbitrary",),
                                             vmem_limit_bytes=VMEM_LIMIT),
        name="ffn",
    )(x2, ada3, w_ab, w_out, ln_g.reshape(1, d), ln_b.reshape(1, d))


def _rope_tables(seq):
    half = ROPE_DIMS // 2
    inv_freq = ROPE_THETA ** (-jnp.arange(half, dtype=F32) * 2.0 / ROPE_DIMS)
    ang = jnp.arange(seq, dtype=F32)[:, None] * inv_freq[None, :]
    cos, sin = jnp.cos(ang), jnp.sin(ang)
    ones = jnp.ones((seq, HEAD_DIM - ROPE_DIMS), F32)
    zeros = jnp.zeros((seq, HEAD_DIM - ROPE_DIMS), F32)
    z8 = jnp.zeros((seq, half), F32)
    cos_h = jnp.concatenate([cos, cos, ones], axis=1)
    sa_h = jnp.concatenate([-sin, z8, zeros], axis=1)
    sb_h = jnp.concatenate([z8, sin, zeros], axis=1)
    rep = lambda t: jnp.concatenate([t] * (LANES // HEAD_DIM), axis=1)
    return rep(cos_h), rep(sa_h), rep(sb_h)


def _mask_tables(seq):
    nb = seq // SEL_BLOCK
    nc = (seq - CMP_BLOCK) // CMP_STRIDE + 1
    ncp = seq // CMP_STRIDE
    cs = np.arange(nc) * CMP_STRIDE
    bs = np.arange(nb) * SEL_BLOCK
    ov = np.minimum(cs[:, None] + CMP_BLOCK, bs[None, :] + SEL_BLOCK) - np.maximum(cs[:, None], bs[None, :])
    ov = np.clip(ov, 0, None) / CMP_BLOCK
    ovt = np.zeros((SEL_BLOCK, ncp), np.float32)
    ovt[:nb, :nc] = ov.T
    tl = np.arange(NSA_LANES) % SEL_BLOCK
    dtab = (np.arange(WIN_KEYS)[:, None] - tl[None, :]).astype(np.int32)
    ctab = (np.arange(ncp)[:, None] * CMP_STRIDE + (CMP_BLOCK - 1) - tl[None, :]).astype(np.int32)
    return jnp.asarray(ovt, BF16), jnp.asarray(dtab), jnp.asarray(ctab)


def _proj_weight(w_in_l):
    d = w_in_l.shape[0]
    sizes = (512, 128, 128, 128, 128, 128, 128, 24, 512, 512, 512, 2 * d)
    offs = np.cumsum((0,) + sizes)
    q_n, kc, vc, ks, vs, kw, vw, g_n, q_d, k_d, v_d, g_m = [w_in_l[:, offs[k]:offs[k + 1]] for k in range(12)]
    qh = q_n.reshape(d, NSA_HEADS, HEAD_DIM)
    z = jnp.zeros_like(qh)
    first = jnp.arange(NSA_HEADS)[None, :, None] < NSA_GROUP
    q_wide = jnp.concatenate([jnp.where(first, qh, z), jnp.where(first, z, qh)], axis=-1).reshape(d, NSA_HEADS * LANES)
    zeros = lambda n: jnp.zeros((d, n), w_in_l.dtype)
    half_slabs = [jnp.concatenate([t[:, g * HEAD_DIM:(g + 1) * HEAD_DIM], zeros(LANES - HEAD_DIM)], axis=1)
                  for t in (kc, vc) for g in range(NSA_KV_HEADS)]
    w = jnp.concatenate([q_wide, q_wide, q_d, k_d, ks, kw, g_n, zeros(COL_KV - COL_GN - g_n.shape[1])]
                        + half_slabs, axis=1)
    assert w.shape[1] == PROJ_COLS
    wtv = jnp.concatenate([v_d, vs, vw], axis=1).T
    scale = np.ones((1, PROJ_COLS), np.float32)
    scale[:, COL_QR:COL_QD + 512] = QK_SCALE
    return w.astype(BF16), wtv.astype(BF16), jnp.asarray(scale), g_m.astype(BF16)


def kernel(x, c, w_ada, b_ada, w_in, cmp_pe_k, cmp_w1_k, cmp_w2_k, cmp_pe_v, cmp_w1_v, cmp_w2_v, diff_lambda, diff_norm_g, w_branch_nsa, w_branch_diff, w_out, ln1_g, ln1_b, w_ffn_in, w_ffn_out, ln2_g, ln2_b):
    bsz, seq, d = x.shape
    depth = w_ada.shape[0]
    assert seq % 1024 == 0, seq
    assert seq // SEL_BLOCK <= SEL_BLOCK
    alpha = (2 * depth) ** 0.25
    ncp = seq // CMP_STRIDE
    nblk = seq // SEL_BLOCK
    cos_t, sa_t, sb_t = _rope_tables(seq)
    ovt, dtab, ctab = _mask_tables(seq)
    onehot = (jnp.arange(seq)[:, None] // SEL_BLOCK == jnp.arange(LANES)[None, :]).astype(BF16)

    x2 = x.reshape(bsz * seq, d)
    for l in range(depth):
        lambda_init = 0.8 - 0.6 * math.exp(-0.3 * l)
        ada3 = _ada(c, w_ada[l], b_ada[l]).reshape(bsz, 6, d)
        w_proj, wtv, colscale, w_gm = _proj_weight(w_in[l])
        proj, gates, vt512, vw128, *kv_raw = _proj(x2, ada3, w_proj, wtv, colscale, cos_t, sa_t, sb_t, seq)
        proj3 = proj.reshape(bsz, seq, PROJ_COLS)
        chunks = [t.reshape(bsz, ncp, CMP_STRIDE * HEAD_DIM) for t in kv_raw]
        w1 = jnp.stack([cmp_w1_k[l], cmp_w1_v[l]]).astype(BF16)
        pe = jnp.stack([cmp_pe_k[l], cmp_pe_v[l]]).reshape(2, 1, CMP_BLOCK * HEAD_DIM)
        pe = jnp.broadcast_to(pe, (2, 8, CMP_BLOCK * HEAD_DIM)).astype(BF16)
        w2k = cmp_w2_k[l].astype(BF16)
        w2v = cmp_w2_v[l].astype(BF16).T
        w2k = jnp.stack([jnp.concatenate([w2k, jnp.zeros_like(w2k)], axis=1),
                         jnp.concatenate([jnp.zeros_like(w2k), w2k], axis=1)])
        w2vt = jnp.stack([jnp.concatenate([w2v, jnp.zeros_like(w2v)], axis=0),
                          jnp.concatenate([jnp.zeros_like(w2v), w2v], axis=0)])
        kcmp, vct = _compress(chunks, w1, pe, w2k, w2vt)

        nstep = nblk // NSA_PAIR
        gates_t = gates.reshape(bsz, nstep, NSA_PAIR, SEL_BLOCK, LANES)[..., :NSA_HEADS * 3]
        gates_t = gates_t.reshape(bsz, nstep, NSA_PAIR, SEL_BLOCK, NSA_KV_HEADS, NSA_GROUP, 3)
        gates_t = gates_t.transpose(0, 1, 6, 4, 2, 5, 3).reshape(bsz, nstep, 3, NSA_LANES)
        gates_t = jnp.pad(gates_t, ((0, 0), (0, 0), (0, 8 - 3), (0, 0)))
        o_nsa = _nsa(proj3, gates_t, onehot, vt512, vw128, kcmp, vct, ovt, dtab, ctab)
        o_diff = _diff(proj3, vt512, diff_lambda[l], diff_norm_g[l], lambda_init)

        x2 = _merge(x2, ada3, o_nsa.reshape(bsz * seq, -1), o_diff.reshape(bsz * seq, -1), w_gm,
                    w_branch_nsa[l].astype(BF16), w_branch_diff[l].astype(BF16), w_out[l].astype(BF16),
                    ln1_g[l], ln1_b[l], seq, alpha)
        x2 = _ffn(x2, ada3, w_ffn_in[l].astype(BF16), w_ffn_out[l].astype(BF16), ln2_g[l], ln2_b[l], seq, alpha)
    return x2.reshape(bsz, seq, d)
```

```python
import functools
import math

import numpy as np
import jax
import jax.numpy as jnp
from jax import lax
from jax.experimental import pallas as pl
from jax.experimental.pallas import tpu as pltpu

F32 = jnp.float32
BF16 = jnp.bfloat16

HEAD_DIM = 64
NSA_HEADS = 8
NSA_KV_HEADS = 2
NSA_GROUP = NSA_HEADS // NSA_KV_HEADS
CMP_BLOCK = 32
CMP_STRIDE = 16
CMP_HIDDEN = 256
SEL_BLOCK = 64
SEL_TOPK = 16
WINDOW = 512
DIFF_HEADS = 4
ROPE_THETA = 500000.0
ROPE_DIMS = HEAD_DIM // 4
EPS = 1e-5
NEG = -1e30
FORCE_SCORE = 1e9
QK_SCALE = HEAD_DIM ** -0.5

LANES = 128
VMEM_LIMIT = 56 * 1024 * 1024

PROJ_TN = 1024
MXU_COLS = 256
COL_QR = 0
COL_QC = 1024
COL_QD = 2048
COL_KD = 2560
COL_KS = 3072
COL_KW = 3200
COL_GN = 3328
COL_KV = 3584
PROJ_COLS = 4096
ROPE_TILES = (0, 2)
PLAIN_TILES = (1,)
LAST_TILE = 3
ROW_VD = 0
ROW_VS = 512
ROW_VW = 640
VT_ROWS_ALL = 768
VT_CHUNK = 512
VW_CHUNK = 128


def _nt(a, b):
    return lax.dot_general(a, b, (((1,), (1,)), ((), ())), preferred_element_type=F32)


def _split_bf16(a):
    hi = a.astype(BF16)
    lo = (a - hi.astype(F32)).astype(BF16)
    return hi, lo


def _layer_norm(h, g, b):
    mu = jnp.mean(h, axis=-1, keepdims=True)
    d = h - mu
    var = jnp.mean(d * d, axis=-1, keepdims=True)
    return d * lax.rsqrt(var + EPS) * g + b


def _ada_kernel(c_ref, w_ref, b_ref, o_ref):
    c = c_ref[...]
    a = c * jax.nn.sigmoid(c)
    a_hi, a_lo = _split_bf16(a)
    w_hi, w_lo = _split_bf16(w_ref[...])
    acc = jnp.dot(a_hi, w_hi, preferred_element_type=F32)
    acc += jnp.dot(a_lo, w_hi, preferred_element_type=F32)
    acc += jnp.dot(a_hi, w_lo, preferred_element_type=F32)
    o_ref[...] = acc + b_ref[...]


def _ada(c, w, b):
    bsz, d = c.shape
    n = w.shape[1]
    tn = 1024
    return pl.pallas_call(
        _ada_kernel,
        grid=(n // tn,),
        in_specs=[pl.BlockSpec((bsz, d), lambda j: (0, 0)),
                  pl.BlockSpec((d, tn), lambda j: (0, j)),
                  pl.BlockSpec((1, tn), lambda j: (0, j))],
        out_specs=pl.BlockSpec((bsz, tn), lambda j: (0, j)),
        out_shape=jax.ShapeDtypeStruct((bsz, n), F32),
        compiler_params=pltpu.CompilerParams(dimension_semantics=("arbitrary",),
                                             vmem_limit_bytes=VMEM_LIMIT),
        name="ada",
    )(c, w, b.reshape(1, n))


def _proj_kernel(x_ref, ada_ref, w_ref, wtv_ref, cs_ref, cos_ref, sa_ref, sb_ref,
                 o_ref, g_ref, vt_ref, vw_ref, kc0_ref, kc1_ref, vc0_ref, vc1_ref, u_ref):
    j = pl.program_id(1)
    tm = u_ref.shape[0]
    tn = w_ref.shape[1]

    @pl.when(j == 0)
    def _():
        sh = ada_ref[0, 0:1, :]
        sc = ada_ref[0, 1:2, :]
        u_ref[...] = (x_ref[...] * (1.0 + sc) + sh).astype(BF16)
        vt = _nt(wtv_ref[...], u_ref[...]).astype(BF16)
        for cc in range(tm // VT_CHUNK):
            vt_ref[cc] = vt[ROW_VD:ROW_VW, cc * VT_CHUNK:(cc + 1) * VT_CHUNK]
        for cc in range(tm // VW_CHUNK):
            vw_ref[cc] = vt[ROW_VW:VT_ROWS_ALL, cc * VW_CHUNK:(cc + 1) * VW_CHUNK]

    def col_tile(ct):
        cols = slice(ct * MXU_COLS, (ct + 1) * MXU_COLS)
        return jnp.dot(u_ref[...], w_ref[:, cols], preferred_element_type=F32) * cs_ref[:, cols]

    def store_rope(a, ct):
        cosv, sav, sbv = cos_ref[...], sa_ref[...], sb_ref[...]
        for s in range(MXU_COLS // LANES):
            a_s = a[:, s * LANES:(s + 1) * LANES]
            r = a_s * cosv + pltpu.roll(a_s, LANES - ROPE_DIMS // 2, 1) * sav + pltpu.roll(a_s, ROPE_DIMS // 2, 1) * sbv
            o_ref[:, ct * MXU_COLS + s * LANES:ct * MXU_COLS + (s + 1) * LANES] = r.astype(BF16)

    def store_plain(a, ct):
        o_ref[:, ct * MXU_COLS:(ct + 1) * MXU_COLS] = a.astype(BF16)

    def any_of(tiles):
        return functools.reduce(jnp.logical_or, [j == t for t in tiles])

    @pl.when(any_of(ROPE_TILES))
    def _():
        for ct in range(tn // MXU_COLS):
            store_rope(col_tile(ct), ct)

    @pl.when(any_of(PLAIN_TILES))
    def _():
        for ct in range(tn // MXU_COLS):
            store_plain(col_tile(ct), ct)

    @pl.when(j == LAST_TILE)
    def _():
        gate_ct = (COL_GN - LAST_TILE * PROJ_TN) // MXU_COLS
        kv_ct = (COL_KV - LAST_TILE * PROJ_TN) // MXU_COLS
        kv_refs = ((kc0_ref, kc1_ref), (vc0_ref, vc1_ref))
        for ct in range(tn // MXU_COLS):
            a = col_tile(ct)
            if ct < kv_ct:
                if ct == gate_ct:
                    g_ref[...] = jax.nn.sigmoid(a[:, 0:LANES])
                store_rope(a, ct)
            else:
                for g, ref in enumerate(kv_refs[ct - kv_ct]):
                    ref[...] = a[:, g * LANES:g * LANES + HEAD_DIM].astype(BF16)
                store_plain(a, ct)


def _proj(x2, ada3, w, wtv, colscale, cos_t, sa_t, sb_t, seq):
    m, d = x2.shape
    tm = min(1024, seq)
    tn = PROJ_TN
    per_seq = seq // tm
    kv_spec = pl.BlockSpec((tm, HEAD_DIM), lambda i, j: (i, 0))
    kv_shape = jax.ShapeDtypeStruct((m, HEAD_DIM), BF16)
    return pl.pallas_call(
        _proj_kernel,
        grid=(m // tm, w.shape[1] // tn),
        in_specs=[pl.BlockSpec((tm, d), lambda i, j: (i, 0)),
                  pl.BlockSpec((1, 6, d), lambda i, j: (i // per_seq, 0, 0)),
                  pl.BlockSpec((d, tn), lambda i, j: (0, j)),
                  pl.BlockSpec(wtv.shape, lambda i, j: (0, 0)),
                  pl.BlockSpec((1, tn), lambda i, j: (0, j)),
                  pl.BlockSpec((tm, LANES), lambda i, j: (i % per_seq, 0)),
                  pl.BlockSpec((tm, LANES), lambda i, j: (i % per_seq, 0)),
                  pl.BlockSpec((tm, LANES), lambda i, j: (i % per_seq, 0))],
        out_specs=[pl.BlockSpec((tm, tn), lambda i, j: (i, j)),
                   pl.BlockSpec((tm, LANES), lambda i, j: (i, 0)),
                   pl.BlockSpec((tm // VT_CHUNK, ROW_VW, VT_CHUNK), lambda i, j: (i, 0, 0)),
                   pl.BlockSpec((tm // VW_CHUNK, VT_ROWS_ALL - ROW_VW, VW_CHUNK), lambda i, j: (i, 0, 0)),
                   kv_spec, kv_spec, kv_spec, kv_spec],
        out_shape=[jax.ShapeDtypeStruct((m, PROJ_COLS), BF16),
                   jax.ShapeDtypeStruct((m, LANES), F32),
                   jax.ShapeDtypeStruct((m // VT_CHUNK, ROW_VW, VT_CHUNK), BF16),
                   jax.ShapeDtypeStruct((m // VW_CHUNK, VT_ROWS_ALL - ROW_VW, VW_CHUNK), BF16),
                   kv_shape, kv_shape, kv_shape, kv_shape],
        scratch_shapes=[pltpu.VMEM((tm, d), BF16)],
        compiler_params=pltpu.CompilerParams(dimension_semantics=("arbitrary", "arbitrary"),
                                             vmem_limit_bytes=VMEM_LIMIT),
        name="proj",
    )(x2, ada3, w, wtv, colscale, cos_t, sa_t, sb_t)


def _gelu_tanh(x):
    return x * (0.5 * (1.0 + jnp.tanh(math.sqrt(2.0 / math.pi) * (x + 0.044715 * (x * x * x)))))


def _compress_kernel(kc0_ref, kc1_ref, vc0_ref, vc1_ref, w1_ref, pe_ref, w2k_ref, w2vt_ref, k_ref, vt_ref):
    xs = ((kc0_ref, kc1_ref), (vc0_ref, vc1_ref))
    nrow = kc0_ref.shape[1]
    half = w1_ref.shape[1] // 2

    def hidden(which, g):
        w1 = w1_ref[which]
        pe_row = jnp.dot(pe_ref[which], w1, preferred_element_type=F32)[0:1, :]
        x = xs[which][g][0]
        a = jnp.dot(x, w1[0:half], preferred_element_type=F32)
        b = jnp.dot(x, w1[half:2 * half], preferred_element_type=F32)
        return _gelu_tanh(a + pltpu.roll(b, nrow - 1, 0) + pe_row).astype(BF16)

    k_out = jnp.zeros((nrow, LANES), F32)
    vt_out = jnp.zeros((LANES, nrow), F32)
    for g in range(NSA_KV_HEADS):
        k_out = k_out + jnp.dot(hidden(0, g), w2k_ref[g], preferred_element_type=F32)
        vt_out = vt_out + _nt(w2vt_ref[g], hidden(1, g))
    k_ref[0] = k_out.astype(BF16)
    vt_ref[0] = vt_out.astype(BF16)


def _compress(chunks, w1, pe, w2k, w2vt):
    bsz, nrow, width = chunks[0].shape
    const = lambda a: pl.BlockSpec(a.shape, lambda b: (0,) * a.ndim)
    chunk_spec = pl.BlockSpec((1, nrow, width), lambda b: (b, 0, 0))
    return pl.pallas_call(
        _compress_kernel,
        grid=(bsz,),
        in_specs=[chunk_spec] * 4 + [const(w1), const(pe), const(w2k), const(w2vt)],
        out_specs=[pl.BlockSpec((1, nrow, LANES), lambda b: (b, 0, 0)),
                   pl.BlockSpec((1, LANES, nrow), lambda b: (b, 0, 0))],
        out_shape=[jax.ShapeDtypeStruct((bsz, nrow, LANES), BF16),
                   jax.ShapeDtypeStruct((bsz, LANES, nrow), BF16)],
        compiler_params=pltpu.CompilerParams(dimension_semantics=("arbitrary",),
                                             vmem_limit_bytes=VMEM_LIMIT),
        name="compress",
    )(*chunks, w1, pe, w2k, w2vt)


NSA_ROWS = NSA_GROUP * SEL_BLOCK
NSA_PAIR = 4
NSA_LANES = NSA_KV_HEADS * NSA_PAIR * NSA_ROWS
SEL_CHUNK = VT_CHUNK
NSA_TOKENS = NSA_PAIR * SEL_BLOCK
WIN_KEYS = WINDOW + NSA_TOKENS
ONES_ROWS = 16


def _with_ones(vt):
    return jnp.concatenate([vt, jnp.ones((ONES_ROWS, vt.shape[1]), vt.dtype)], axis=0)


def _nsa_kernel(qr_ref, qc_ref, ks_ref, oh_ref, vs_ref, kw_ref, vw_ref, kc_ref, vct_ref, gate_ref,
                ovt_ref, dtab_ref, ctab_ref, o_ref, m_ref, acc_ref, s_ref):
    ip = pl.program_id(1)
    glanes = NSA_PAIR * NSA_ROWS
    assert NSA_TOKENS % VW_CHUNK == 0 and SEL_CHUNK % NSA_TOKENS == 0

    def stack_heads(ref):
        return jnp.concatenate(
            [ref[0, blk * SEL_BLOCK:(blk + 1) * SEL_BLOCK, (g * NSA_GROUP + r) * LANES:(g * NSA_GROUP + r + 1) * LANES]
             for g in range(NSA_KV_HEADS) for blk in range(NSA_PAIR) for r in range(NSA_GROUP)], axis=0)

    def pv(vts, p):
        return jnp.concatenate(
            [jnp.dot(_with_ones(vts[g * HEAD_DIM:(g + 1) * HEAD_DIM]), p[:, g * glanes:(g + 1) * glanes],
                     preferred_element_type=F32) for g in range(NSA_KV_HEADS)], axis=1)

    def block_of(lane_iota, width):
        return NSA_PAIR * ip + ((lane_iota // width) % NSA_PAIR)

    lane128 = lax.broadcasted_iota(jnp.int32, (SEL_BLOCK, LANES), 1)
    tok128 = lax.broadcasted_iota(jnp.int32, (SEL_BLOCK, LANES), 0)
    lower = lane128 < HEAD_DIM
    gates = gate_ref[0, 0]
    i_row = block_of(lax.broadcasted_iota(jnp.int32, (1, NSA_LANES), 1), NSA_ROWS)
    t0_row = i_row * SEL_BLOCK

    qc = stack_heads(qc_ref)
    ctab = ctab_ref[...]
    any_valid = jnp.where(ctab[0:1] <= t0_row, 1.0, 0.0)
    s = jnp.where(ctab <= t0_row, _nt(kc_ref[0], qc), NEG)
    e = jnp.exp(s - jnp.max(s, axis=0, keepdims=True))
    e_hi, e_lo = _split_bf16(e)
    acc = pv(vct_ref[0], e_hi)
    inv = any_valid / acc[HEAD_DIM:HEAD_DIM + 1]
    out = (gates[0:1] * inv) * acc[0:HEAD_DIM]
    imp4 = (jnp.dot(ovt_ref[...], e_hi, preferred_element_type=F32)
            + jnp.dot(ovt_ref[...], e_lo, preferred_element_type=F32)) * inv
    imps = {}
    for g in range(NSA_KV_HEADS):
        for blk in range(NSA_PAIR):
            c0 = (g * NSA_PAIR + blk) * NSA_ROWS
            two = imp4[:, c0:c0 + LANES] + imp4[:, c0 + LANES:c0 + 2 * LANES]
            imps[g, blk] = two + pltpu.roll(two, HEAD_DIM, 1)
    imp_t = jnp.concatenate([jnp.where(lower, imps[0, blk], imps[1, blk]) for blk in range(NSA_PAIR)],
                            axis=1)

    qr = stack_heads(qr_ref)
    nwin = WIN_KEYS // VW_CHUNK
    wb0 = jnp.maximum(ip * (NSA_TOKENS // VW_CHUNK) - WINDOW // VW_CHUNK, 0)
    wrows = pl.ds(pl.multiple_of(wb0 * VW_CHUNK, VW_CHUNK), WIN_KEYS)
    dist = (t0_row - wb0 * VW_CHUNK) - dtab_ref[...]
    s = jnp.where(lax.bitcast_convert_type(dist, jnp.uint32) < WINDOW, _nt(kw_ref[0, wrows, :], qr), NEG)
    e = jnp.exp(s - jnp.max(s, axis=0, keepdims=True)).astype(BF16)
    acc = pv(jnp.concatenate([vw_ref[wb0 + n] for n in range(nwin)], axis=1), e)
    out = out + (gates[2:3] * (1.0 / acc[HEAD_DIM:HEAD_DIM + 1])) * acc[0:HEAD_DIM]

    nb_pad, nsel = imp_t.shape
    jj = lax.broadcasted_iota(jnp.int32, (nb_pad, nsel), 0)
    i_sel = block_of(lax.broadcasted_iota(jnp.int32, (1, nsel), 1), LANES)
    forced = (jj == 0) | (jj == i_sel) | (jj == i_sel - 1)
    val = jnp.where(jj > i_sel, -jnp.inf, jnp.where(forced, FORCE_SCORE, imp_t))
    sub = 8
    blocks = [val[v * sub:(v + 1) * sub, :] for v in range(nb_pad // sub)]
    ranks = [jnp.zeros((sub, nsel), F32) for _ in blocks]
    jloc = lax.broadcasted_iota(jnp.int32, (sub, nsel), 0)
    for k in range(nb_pad):
        rowk = jnp.broadcast_to(val[k:k + 1, :], (sub, nsel))
        for v in range(nb_pad // sub):
            if v * sub > k:
                beats = jnp.where(rowk >= blocks[v], 1.0, 0.0)
            elif (v + 1) * sub <= k:
                beats = jnp.where(rowk > blocks[v], 1.0, 0.0)
            else:
                beats = jnp.where(jloc > k - v * sub,
                                  jnp.where(rowk >= blocks[v], 1.0, 0.0),
                                  jnp.where(rowk > blocks[v], 1.0, 0.0))
            ranks[v] = ranks[v] + beats
    rank = jnp.concatenate(ranks, axis=0)
    sel_bias_t = jnp.where(rank < float(SEL_TOPK), jnp.where(jj <= i_sel, 0.0, NEG), NEG).astype(BF16)
    xg = jnp.concatenate([sel_bias_t, jnp.zeros_like(sel_bias_t)], axis=0)
    lane_sel = lax.broadcasted_iota(jnp.int32, (SEL_BLOCK, nsel), 1)
    tok_sel = lax.broadcasted_iota(jnp.int32, (SEL_BLOCK, nsel), 0)
    bias_rows = []
    for g in range(NSA_KV_HEADS):
        for blk in range(NSA_PAIR):
            pick = jnp.where(lane_sel == tok_sel + blk * LANES + g * SEL_BLOCK, 1.0, 0.0).astype(BF16)
            bias_rows += [_nt(pick, xg).astype(BF16)] * NSA_GROUP
    q_aug = jnp.concatenate([qr, jnp.concatenate(bias_rows, axis=0)], axis=1)

    m_ref[...] = jnp.full(m_ref.shape, NEG, F32)
    acc_ref[...] = jnp.zeros(acc_ref.shape, F32)
    per_chunk = SEL_CHUNK // SEL_BLOCK
    nfull = (NSA_PAIR * ip) // per_chunk

    def scores(c, slot):
        rows = pl.ds(pl.multiple_of(c * SEL_CHUNK, SEL_CHUNK), SEL_CHUNK)
        k_aug = jnp.concatenate([ks_ref[0, rows, :], oh_ref[rows, :]], axis=1)
        s_ref[slot] = _nt(k_aug, q_aug)

    def absorb(c, slot, diag):
        s = s_ref[slot]
        if diag:
            s = jnp.where(dtab_ref[0:SEL_CHUNK, :] > (i_row % per_chunk) * SEL_BLOCK, NEG, s)
        m_old = m_ref[...]
        m_new = jnp.maximum(m_old, jnp.max(s, axis=0, keepdims=True))
        p = jnp.exp(s - m_new).astype(BF16)
        acc_ref[...] = jnp.exp(m_old - m_new) * acc_ref[...] + pv(vs_ref[c], p)
        m_ref[...] = m_new

    scores(0, 0)

    def pair(jp, carry):
        c = 2 * jp
        scores(c + 1, 1)
        absorb(c, 0, False)
        scores(c + 2, 0)
        absorb(c + 1, 1, False)
        return carry

    lax.fori_loop(0, nfull // 2, pair, 0)

    @pl.when(nfull % 2 == 1)
    def _():
        scores(nfull, 1)
        absorb(nfull - 1, 0, False)
        absorb(nfull, 1, True)

    @pl.when(nfull % 2 == 0)
    def _():
        absorb(nfull, 0, True)

    acc = acc_ref[...]
    out = (out + (gates[1:2] * (1.0 / acc[HEAD_DIM:HEAD_DIM + 1])) * acc[0:HEAD_DIM]).astype(BF16)

    fold = jnp.where((lane128 & (SEL_BLOCK - 1)) == tok128, 1.0, 0.0).astype(BF16)
    zero = jnp.zeros((HEAD_DIM, LANES), BF16)
    for blk in range(NSA_PAIR):
        for k in range(NSA_HEADS // 2):
            g, half = divmod(k, NSA_GROUP // 2)
            c0 = (g * NSA_PAIR + blk) * NSA_ROWS + half * LANES
            x = out[:, c0:c0 + LANES]
            y = jnp.concatenate([jnp.where(lower, x, zero), jnp.where(lower, zero, x)], axis=0)
            o_ref[0, blk * SEL_BLOCK:(blk + 1) * SEL_BLOCK, k * LANES:(k + 1) * LANES] = _nt(fold, y).astype(BF16)


def _nsa(proj3, gates_t, onehot, vt512, vw128, kcmp, vct, ovt, dtab, ctab):
    bsz, seq, _ = proj3.shape
    nstep = seq // NSA_TOKENS
    tq = NSA_TOKENS
    qw = NSA_HEADS * LANES
    const = lambda a: pl.BlockSpec(a.shape, lambda b, i: (0,) * a.ndim)
    per_b = lambda a: pl.BlockSpec((1,) + a.shape[1:], lambda b, i: (b,) + (0,) * (a.ndim - 1))
    return pl.pallas_call(
        _nsa_kernel,
        grid=(bsz, nstep),
        in_specs=[pl.BlockSpec((1, tq, qw), lambda b, i: (b, i, COL_QR // qw)),
                  pl.BlockSpec((1, tq, qw), lambda b, i: (b, i, COL_QC // qw)),
                  pl.BlockSpec((1, seq, LANES), lambda b, i: (b, 0, COL_KS // LANES)),
                  const(onehot),
                  pl.BlockSpec((seq // VT_CHUNK, LANES, VT_CHUNK), lambda b, i: (b, ROW_VS // LANES, 0)),
                  pl.BlockSpec((1, seq, LANES), lambda b, i: (b, 0, COL_KW // LANES)),
                  pl.BlockSpec((seq // VW_CHUNK, LANES, VW_CHUNK), lambda b, i: (b, 0, 0)),
                  per_b(kcmp), per_b(vct),
                  pl.BlockSpec((1, 1) + gates_t.shape[2:], lambda b, i: (b, i, 0, 0)),
                  const(ovt), const(dtab), const(ctab)],
        out_specs=pl.BlockSpec((1, tq, NSA_HEADS * HEAD_DIM), lambda b, i: (b, i, 0)),
        out_shape=jax.ShapeDtypeStruct((bsz, seq, NSA_HEADS * HEAD_DIM), BF16),
        scratch_shapes=[pltpu.VMEM((1, NSA_LANES), F32), pltpu.VMEM((HEAD_DIM + ONES_ROWS, NSA_LANES), F32),
                        pltpu.VMEM((2, SEL_CHUNK, NSA_LANES), F32)],
        compiler_params=pltpu.CompilerParams(dimension_semantics=("arbitrary", "arbitrary"),
                                             vmem_limit_bytes=VMEM_LIMIT),
        name="nsa",
    )(proj3, proj3, proj3, onehot, vt512, proj3, vw128, kcmp, vct, gates_t, ovt, dtab, ctab)


DIFF_TQ = VT_CHUNK
DIFF_PAIR = 2


def _diff_kernel(q_ref, k_ref, vt_ref, lam_ref, ng_ref, o_ref, m_ref, acc_ref, s_ref, *, lambda_init):
    i = pl.program_id(2)
    tq = DIFF_TQ
    nv = 2 * HEAD_DIM
    hl = 2 * tq
    lane = lax.broadcasted_iota(jnp.int32, (tq, LANES), 1)
    q_both = []
    for hh in range(DIFF_PAIR):
        q = q_ref[0, :, hh * LANES:(hh + 1) * LANES]
        zero = jnp.zeros_like(q)
        q_both.append(jnp.concatenate([jnp.where(lane < HEAD_DIM, q, zero), jnp.where(lane >= HEAD_DIM, q, zero)],
                                      axis=0))
    m_ref[...] = jnp.full(m_ref.shape, NEG, F32)
    acc_ref[...] = jnp.zeros(acc_ref.shape, F32)

    def scores(c, slot):
        rows = pl.ds(pl.multiple_of(c * tq, tq), tq)
        for hh in range(DIFF_PAIR):
            s_ref[slot, :, hh * hl:(hh + 1) * hl] = _nt(k_ref[0, rows, hh * LANES:(hh + 1) * LANES], q_both[hh])

    def absorb(c, slot, causal):
        s = s_ref[slot]
        if causal:
            kpos = lax.broadcasted_iota(jnp.int32, s.shape, 0)
            qpos = lax.broadcasted_iota(jnp.int32, s.shape, 1) & (tq - 1)
            s = jnp.where(kpos <= qpos, s, NEG)
        m_old = m_ref[...]
        m_new = jnp.maximum(m_old, jnp.max(s, axis=0, keepdims=True))
        p = jnp.exp(s - m_new).astype(BF16)
        vt = vt_ref[c]
        pv = jnp.concatenate([jnp.dot(_with_ones(vt[hh * nv:(hh + 1) * nv]), p[:, hh * hl:(hh + 1) * hl],
                                      preferred_element_type=F32) for hh in range(DIFF_PAIR)], axis=1)
        acc_ref[...] = jnp.exp(m_old - m_new) * acc_ref[...] + pv
        m_ref[...] = m_new

    scores(0, 0)

    def pair(jp, carry):
        c = 2 * jp
        scores(c + 1, 1)
        absorb(c, 0, False)
        scores(c + 2, 0)
        absorb(c + 1, 1, False)
        return carry

    lax.fori_loop(0, i // 2, pair, 0)

    @pl.when(i % 2 == 1)
    def _():
        scores(i, 1)
        absorb(i - 1, 0, False)
        absorb(i, 1, True)

    @pl.when(i % 2 == 0)
    def _():
        absorb(i, 0, True)

    lv = lam_ref[...]
    lam = (jnp.exp(jnp.sum(lv[0:1] * lv[1:2], axis=1, keepdims=True))
           - jnp.exp(jnp.sum(lv[2:3] * lv[3:4], axis=1, keepdims=True)) + lambda_init)
    acc = acc_ref[...]
    on = acc[0:nv] * (1.0 / acc[nv:nv + 1])
    for hh in range(DIFF_PAIR):
        o = on[:, hh * hl:hh * hl + tq] - lam * on[:, hh * hl + tq:(hh + 1) * hl]
        o = o * lax.rsqrt(jnp.mean(o * o, axis=0, keepdims=True) + EPS) * ng_ref[...] * (1.0 - lambda_init)
        o_ref[0, :, hh * LANES:(hh + 1) * LANES] = o.T.astype(BF16)


def _diff(proj3, vt512, lam_vec, norm_g, lambda_init):
    bsz, seq, _ = proj3.shape
    tq = DIFF_TQ
    pw = DIFF_PAIR * LANES
    lanes = DIFF_PAIR * 2 * tq
    kern = functools.partial(_diff_kernel, lambda_init=lambda_init)
    ng = jnp.broadcast_to(norm_g.reshape(LANES, 1), (LANES, tq))
    return pl.pallas_call(
        kern,
        grid=(bsz, DIFF_HEADS // DIFF_PAIR, seq // tq),
        in_specs=[pl.BlockSpec((1, tq, pw), lambda b, h, i: (b, i, COL_QD // pw + h)),
                  pl.BlockSpec((1, seq, pw), lambda b, h, i: (b, 0, COL_KD // pw + h)),
                  pl.BlockSpec((seq // VT_CHUNK, pw, VT_CHUNK), lambda b, h, i: (b, ROW_VD // pw + h, 0)),
                  pl.BlockSpec(lam_vec.shape, lambda b, h, i: (0, 0)),
                  pl.BlockSpec((LANES, tq), lambda b, h, i: (0, 0))],
        out_specs=pl.BlockSpec((1, tq, pw), lambda b, h, i: (b, i, h)),
        out_shape=jax.ShapeDtypeStruct((bsz, seq, DIFF_HEADS * LANES), BF16),
        scratch_shapes=[pltpu.VMEM((1, lanes), F32), pltpu.VMEM((2 * HEAD_DIM + ONES_ROWS, lanes), F32),
                        pltpu.VMEM((2, tq, lanes), F32)],
        compiler_params=pltpu.CompilerParams(dimension_semantics=("arbitrary", "arbitrary", "arbitrary"),
                                             vmem_limit_bytes=VMEM_LIMIT),
        name="diff",
    )(proj3, proj3, vt512, lam_vec, ng)


MERGE_TILE = 256


def _merge_kernel(x_ref, ada_ref, on_ref, od_ref, wg_ref, wbn_ref, wbd_ref, wo_ref, g_ref, b_ref, o_ref, *, alpha):
    x = x_ref[...]
    sh = ada_ref[0, 0:1, :]
    sc = ada_ref[0, 1:2, :]
    gate = ada_ref[0, 2:3, :]
    u = (x * (1.0 + sc) + sh).astype(BF16)
    o_nsa = on_ref[...]
    o_diff = od_ref[...]
    mix = jnp.zeros(x.shape, F32)
    for c in range(x.shape[1] // MERGE_TILE):
        cols = slice(c * MERGE_TILE, (c + 1) * MERGE_TILE)
        gm = jax.nn.sigmoid(jnp.dot(u, wg_ref[:, 2 * c * MERGE_TILE:2 * (c + 1) * MERGE_TILE],
                                    preferred_element_type=F32))
        y_nsa = jnp.dot(o_nsa, wbn_ref[:, cols], preferred_element_type=F32)
        y_diff = jnp.dot(o_diff, wbd_ref[:, cols], preferred_element_type=F32)
        mixed = (gm[:, 0:MERGE_TILE] * y_nsa + gm[:, MERGE_TILE:2 * MERGE_TILE] * y_diff).astype(BF16)
        mix = mix + jnp.dot(mixed, wo_ref[cols, :], preferred_element_type=F32)
    o_ref[...] = _layer_norm(alpha * x + (1.0 + gate) * mix, g_ref[...], b_ref[...])


def _merge(x2, ada3, o_nsa, o_diff, w_gm, w_bn, w_bd, w_o, ln_g, ln_b, seq, alpha):
    m, d = x2.shape
    tm = min(1024, seq)
    per_seq = seq // tm
    w_g2 = jnp.concatenate([w_gm[:, half * d + c * MERGE_TILE:half * d + (c + 1) * MERGE_TILE]
                            for c in range(d // MERGE_TILE) for half in range(2)], axis=1)
    resident = lambda a: pl.BlockSpec(a.shape, lambda i: (0,) * a.ndim, pipeline_mode=pl.Buffered(1))
    kern = functools.partial(_merge_kernel, alpha=alpha)
    return pl.pallas_call(
        kern,
        grid=(m // tm,),
        in_specs=[pl.BlockSpec((tm, d), lambda i: (i, 0)),
                  pl.BlockSpec((1, 6, d), lambda i: (i // per_seq, 0, 0)),
                  pl.BlockSpec((tm, o_nsa.shape[1]), lambda i: (i, 0)),
                  pl.BlockSpec((tm, o_diff.shape[1]), lambda i: (i, 0)),
                  resident(w_g2), resident(w_bn), resident(w_bd), resident(w_o),
                  pl.BlockSpec((1, d), lambda i: (0, 0)),
                  pl.BlockSpec((1, d), lambda i: (0, 0))],
        out_specs=pl.BlockSpec((tm, d), lambda i: (i, 0)),
        out_shape=jax.ShapeDtypeStruct((m, d), F32),
        compiler_params=pltpu.CompilerParams(dimension_semantics=("arbitrary",),
                                             vmem_limit_bytes=VMEM_LIMIT),
        name="merge",
    )(x2, ada3, o_nsa, o_diff, w_g2, w_bn, w_bd, w_o, ln_g.reshape(1, d), ln_b.reshape(1, d))


FFN_TILE = 256


def _ffn_kernel(x_ref, ada_ref, wab_ref, wo_ref, g_ref, b_ref, o_ref, *, alpha):
    x = x_ref[...]
    sh = ada_ref[0, 3:4, :]
    sc = ada_ref[0, 4:5, :]
    gate = ada_ref[0, 5:6, :]
    u = (x * (1.0 + sc) + sh).astype(BF16)
    acc = jnp.zeros(x.shape, F32)
    for c in range(wo_ref.shape[0] // FFN_TILE):
        ab = jnp.dot(u, wab_ref[:, 2 * c * FFN_TILE:2 * (c + 1) * FFN_TILE], preferred_element_type=F32)
        a = ab[:, 0:FFN_TILE]
        h = ((a * jax.nn.sigmoid(a)) * ab[:, FFN_TILE:2 * FFN_TILE]).astype(BF16)
        acc = acc + jnp.dot(h, wo_ref[c * FFN_TILE:(c + 1) * FFN_TILE, :], preferred_element_type=F32)
    o_ref[...] = _layer_norm(alpha * x + (1.0 + gate) * acc, g_ref[...], b_ref[...])


def _ffn(x2, ada3, w_in, w_out, ln_g, ln_b, seq, alpha):
    m, d = x2.shape
    dff = w_out.shape[0]
    tm = min(1024, seq)
    per_seq = seq // tm
    nt = dff // FFN_TILE
    w_ab = jnp.concatenate([w_in[:, half * dff + c * FFN_TILE:half * dff + (c + 1) * FFN_TILE]
                            for c in range(nt) for half in range(2)], axis=1)
    resident = lambda a: pl.BlockSpec(a.shape, lambda i: (0,) * a.ndim, pipeline_mode=pl.Buffered(1))
    kern = functools.partial(_ffn_kernel, alpha=alpha)
    return pl.pallas_call(
        kern,
        grid=(m // tm,),
        in_specs=[pl.BlockSpec((tm, d), lambda i: (i, 0)),
                  pl.BlockSpec((1, 6, d), lambda i: (i // per_seq, 0, 0)),
                  resident(w_ab), resident(w_out),
                  pl.BlockSpec((1, d), lambda i: (0, 0)),
                  pl.BlockSpec((1, d), lambda i: (0, 0))],
        out_specs=pl.BlockSpec((tm, d), lambda i: (i, 0)),
        out_shape=jax.ShapeDtypeStruct((m, d), F32),
        compiler_params=pltpu.CompilerParams(dimension_semantics=("arbitrary",),
                                             vmem_limit_bytes=VMEM_LIMIT),
        name="ffn",
    )(x2, ada3, w_ab, w_out, ln_g.reshape(1, d), ln_b.reshape(1, d))


def _rope_tables(seq):
    half = ROPE_DIMS // 2
    inv_freq = ROPE_THETA ** (-jnp.arange(half, dtype=F32) * 2.0 / ROPE_DIMS)
    ang = jnp.arange(seq, dtype=F32)[:, None] * inv_freq[None, :]
    cos, sin = jnp.cos(ang), jnp.sin(ang)
    ones = jnp.ones((seq, HEAD_DIM - ROPE_DIMS), F32)
    zeros = jnp.zeros((seq, HEAD_DIM - ROPE_DIMS), F32)
    z8 = jnp.zeros((seq, half), F32)
    cos_h = jnp.concatenate([cos, cos, ones], axis=1)
    sa_h = jnp.concatenate([-sin, z8, zeros], axis=1)
    sb_h = jnp.concatenate([z8, sin, zeros], axis=1)
    rep = lambda t: jnp.concatenate([t] * (LANES // HEAD_DIM), axis=1)
    return rep(cos_h), rep(sa_h), rep(sb_h)


def _mask_tables(seq):
    nb = seq // SEL_BLOCK
    nc = (seq - CMP_BLOCK) // CMP_STRIDE + 1
    ncp = seq // CMP_STRIDE
    cs = np.arange(nc) * CMP_STRIDE
    bs = np.arange(nb) * SEL_BLOCK
    ov = np.minimum(cs[:, None] + CMP_BLOCK, bs[None, :] + SEL_BLOCK) - np.maximum(cs[:, None], bs[None, :])
    ov = np.clip(ov, 0, None) / CMP_BLOCK
    ovt = np.zeros((SEL_BLOCK, ncp), np.float32)
    ovt[:nb, :nc] = ov.T
    tl = np.arange(NSA_LANES) % SEL_BLOCK
    dtab = (np.arange(WIN_KEYS)[:, None] - tl[None, :]).astype(np.int32)
    ctab = (np.arange(ncp)[:, None] * CMP_STRIDE + (CMP_BLOCK - 1) - tl[None, :]).astype(np.int32)
    return jnp.asarray(ovt, BF16), jnp.asarray(dtab), jnp.asarray(ctab)


def _proj_weight(w_in_l):
    d = w_in_l.shape[0]
    sizes = (512, 128, 128, 128, 128, 128, 128, 24, 512, 512, 512, 2 * d)
    offs = np.cumsum((0,) + sizes)
    q_n, kc, vc, ks, vs, kw, vw, g_n, q_d, k_d, v_d, g_m = [w_in_l[:, offs[k]:offs[k + 1]] for k in range(12)]
    qh = q_n.reshape(d, NSA_HEADS, HEAD_DIM)
    z = jnp.zeros_like(qh)
    first = jnp.arange(NSA_HEADS)[None, :, None] < NSA_GROUP
    q_wide = jnp.concatenate([jnp.where(first, qh, z), jnp.where(first, z, qh)], axis=-1).reshape(d, NSA_HEADS * LANES)
    zeros = lambda n: jnp.zeros((d, n), w_in_l.dtype)
    half_slabs = [jnp.concatenate([t[:, g * HEAD_DIM:(g + 1) * HEAD_DIM], zeros(LANES - HEAD_DIM)], axis=1)
                  for t in (kc, vc) for g in range(NSA_KV_HEADS)]
    w = jnp.concatenate([q_wide, q_wide, q_d, k_d, ks, kw, g_n, zeros(COL_KV - COL_GN - g_n.shape[1])]
                        + half_slabs, axis=1)
    assert w.shape[1] == PROJ_COLS
    wtv = jnp.concatenate([v_d, vs, vw], axis=1).T
    scale = np.ones((1, PROJ_COLS), np.float32)
    scale[:, COL_QR:COL_QD + 512] = QK_SCALE
    return w.astype(BF16), wtv.astype(BF16), jnp.asarray(scale), g_m.astype(BF16)


def kernel(x, c, w_ada, b_ada, w_in, cmp_pe_k, cmp_w1_k, cmp_w2_k, cmp_pe_v, cmp_w1_v, cmp_w2_v, diff_lambda, diff_norm_g, w_branch_nsa, w_branch_diff, w_out, ln1_g, ln1_b, w_ffn_in, w_ffn_out, ln2_g, ln2_b):
    bsz, seq, d = x.shape
    depth = w_ada.shape[0]
    assert seq % 1024 == 0, seq
    assert seq // SEL_BLOCK <= SEL_BLOCK
    alpha = (2 * depth) ** 0.25
    ncp = seq // CMP_STRIDE
    nblk = seq // SEL_BLOCK
    cos_t, sa_t, sb_t = _rope_tables(seq)
    ovt, dtab, ctab = _mask_tables(seq)
    onehot = (jnp.arange(seq)[:, None] // SEL_BLOCK == jnp.arange(LANES)[None, :]).astype(BF16)

    x2 = x.reshape(bsz * seq, d)
    for l in range(depth):
        lambda_init = 0.8 - 0.6 * math.exp(-0.3 * l)
        ada3 = _ada(c, w_ada[l], b_ada[l]).reshape(bsz, 6, d)
        w_proj, wtv, colscale, w_gm = _proj_weight(w_in[l])
        proj, gates, vt512, vw128, *kv_raw = _proj(x2, ada3, w_proj, wtv, colscale, cos_t, sa_t, sb_t, seq)
        proj3 = proj.reshape(bsz, seq, PROJ_COLS)
        chunks = [t.reshape(bsz, ncp, CMP_STRIDE * HEAD_DIM) for t in kv_raw]
        w1 = jnp.stack([cmp_w1_k[l], cmp_w1_v[l]]).astype(BF16)
        pe = jnp.stack([cmp_pe_k[l], cmp_pe_v[l]]).reshape(2, 1, CMP_BLOCK * HEAD_DIM)
        pe = jnp.broadcast_to(pe, (2, 8, CMP_BLOCK * HEAD_DIM)).astype(BF16)
        w2k = cmp_w2_k[l].astype(BF16)
        w2v = cmp_w2_v[l].astype(BF16).T
        w2k = jnp.stack([jnp.concatenate([w2k, jnp.zeros_like(w2k)], axis=1),
                         jnp.concatenate([jnp.zeros_like(w2k), w2k], axis=1)])
        w2vt = jnp.stack([jnp.concatenate([w2v, jnp.zeros_like(w2v)], axis=0),
                          jnp.concatenate([jnp.zeros_like(w2v), w2v], axis=0)])
        kcmp, vct = _compress(chunks, w1, pe, w2k, w2vt)

        nstep = nblk // NSA_PAIR
        gates_t = gates.reshape(bsz, nstep, NSA_PAIR, SEL_BLOCK, LANES)[..., :NSA_HEADS * 3]
        gates_t = gates_t.reshape(bsz, nstep, NSA_PAIR, SEL_BLOCK, NSA_KV_HEADS, NSA_GROUP, 3)
        gates_t = gates_t.transpose(0, 1, 6, 4, 2, 5, 3).reshape(bsz, nstep, 3, NSA_LANES)
        gates_t = jnp.pad(gates_t, ((0, 0), (0, 0), (0, 8 - 3), (0, 0)))
        o_nsa = _nsa(proj3, gates_t, onehot, vt512, vw128, kcmp, vct, ovt, dtab, ctab)
        o_diff = _diff(proj3, vt512, diff_lambda[l], diff_norm_g[l], lambda_init)

        x2 = _merge(x2, ada3, o_nsa.reshape(bsz * seq, -1), o_diff.reshape(bsz * seq, -1), w_gm,
                    w_branch_nsa[l].astype(BF16), w_branch_diff[l].astype(BF16), w_out[l].astype(BF16),
                    ln1_g[l], ln1_b[l], seq, alpha)
        x2 = _ffn(x2, ada3, w_ffn_in[l].astype(BF16), w_ffn_out[l].astype(BF16), ln2_g[l], ln2_b[l], seq, alpha)
    return x2.reshape(bsz, seq, d)
```

```python
import functools
import math

import numpy as np
import jax
import jax.numpy as jnp
from jax import lax
from jax.experimental import pallas as pl
from jax.experimental.pallas import tpu as pltpu

F32 = jnp.float32
BF16 = jnp.bfloat16

HEAD_DIM = 64
NSA_HEADS = 8
NSA_KV_HEADS = 2
NSA_GROUP = NSA_HEADS // NSA_KV_HEADS
CMP_BLOCK = 32
CMP_STRIDE = 16
CMP_HIDDEN = 256
SEL_BLOCK = 64
SEL_TOPK = 16
WINDOW = 512
DIFF_HEADS = 4
ROPE_THETA = 500000.0
ROPE_DIMS = HEAD_DIM // 4
EPS = 1e-5
NEG = -1e30
FORCE_SCORE = 1e9
QK_SCALE = HEAD_DIM ** -0.5

LANES = 128
VMEM_LIMIT = 56 * 1024 * 1024

PROJ_TN = 1024
MXU_COLS = 256
COL_QR = 0
COL_QC = 1024
COL_QD = 2048
COL_KD = 2560
COL_KS = 3072
COL_KW = 3200
COL_GN = 3328
COL_KV = 3584
PROJ_COLS = 4096
ROPE_TILES = (0, 2)
PLAIN_TILES = (1,)
LAST_TILE = 3
ROW_VD = 0
ROW_VS = 512
ROW_VW = 640
VT_ROWS_ALL = 768
VT_CHUNK = 512
VW_CHUNK = 128


def _nt(a, b):
    return lax.dot_general(a, b, (((1,), (1,)), ((), ())), preferred_element_type=F32)


def _split_bf16(a):
    hi = a.astype(BF16)
    lo = (a - hi.astype(F32)).astype(BF16)
    return hi, lo


def _layer_norm(h, g, b):
    mu = jnp.mean(h, axis=-1, keepdims=True)
    d = h - mu
    var = jnp.mean(d * d, axis=-1, keepdims=True)
    return d * lax.rsqrt(var + EPS) * g + b


def _ada_kernel(c_ref, w_ref, b_ref, o_ref):
    c = c_ref[...]
    a = c * jax.nn.sigmoid(c)
    a_hi, a_lo = _split_bf16(a)
    w_hi, w_lo = _split_bf16(w_ref[...])
    acc = jnp.dot(a_hi, w_hi, preferred_element_type=F32)
    acc += jnp.dot(a_lo, w_hi, preferred_element_type=F32)
    acc += jnp.dot(a_hi, w_lo, preferred_element_type=F32)
    o_ref[...] = acc + b_ref[...]


def _ada(c, w, b):
    bsz, d = c.shape
    n = w.shape[1]
    tn = 1024
    return pl.pallas_call(
        _ada_kernel,
        grid=(n // tn,),
        in_specs=[pl.BlockSpec((bsz, d), lambda j: (0, 0)),
                  pl.BlockSpec((d, tn), lambda j: (0, j)),
                  pl.BlockSpec((1, tn), lambda j: (0, j))],
        out_specs=pl.BlockSpec((bsz, tn), lambda j: (0, j)),
        out_shape=jax.ShapeDtypeStruct((bsz, n), F32),
        compiler_params=pltpu.CompilerParams(dimension_semantics=("arbitrary",),
                                             vmem_limit_bytes=VMEM_LIMIT),
        name="ada",
    )(c, w, b.reshape(1, n))


def _proj_kernel(x_ref, ada_ref, w_ref, wtv_ref, cs_ref, cos_ref, sa_ref, sb_ref,
                 o_ref, g_ref, vt_ref, vw_ref, kc0_ref, kc1_ref, vc0_ref, vc1_ref, u_ref):
    j = pl.program_id(1)
    tm = u_ref.shape[0]
    tn = w_ref.shape[1]

    @pl.when(j == 0)
    def _():
        sh = ada_ref[0, 0:1, :]
        sc = ada_ref[0, 1:2, :]
        u_ref[...] = (x_ref[...] * (1.0 + sc) + sh).astype(BF16)
        vt = _nt(wtv_ref[...], u_ref[...]).astype(BF16)
        for cc in range(tm // VT_CHUNK):
            vt_ref[cc] = vt[ROW_VD:ROW_VW, cc * VT_CHUNK:(cc + 1) * VT_CHUNK]
        for cc in range(tm // VW_CHUNK):
            vw_ref[cc] = vt[ROW_VW:VT_ROWS_ALL, cc * VW_CHUNK:(cc + 1) * VW_CHUNK]

    def col_tile(ct):
        cols = slice(ct * MXU_COLS, (ct + 1) * MXU_COLS)
        return jnp.dot(u_ref[...], w_ref[:, cols], preferred_element_type=F32) * cs_ref[:, cols]

    def store_rope(a, ct):
        cosv, sav, sbv = cos_ref[...], sa_ref[...], sb_ref[...]
        for s in range(MXU_COLS // LANES):
            a_s = a[:, s * LANES:(s + 1) * LANES]
            r = a_s * cosv + pltpu.roll(a_s, LANES - ROPE_DIMS // 2, 1) * sav + pltpu.roll(a_s, ROPE_DIMS // 2, 1) * sbv
            o_ref[:, ct * MXU_COLS + s * LANES:ct * MXU_COLS + (s + 1) * LANES] = r.astype(BF16)

    def store_plain(a, ct):
        o_ref[:, ct * MXU_COLS:(ct + 1) * MXU_COLS] = a.astype(BF16)

    def any_of(tiles):
        return functools.reduce(jnp.logical_or, [j == t for t in tiles])

    @pl.when(any_of(ROPE_TILES))
    def _():
        for ct in range(tn // MXU_COLS):
            store_rope(col_tile(ct), ct)

    @pl.when(any_of(PLAIN_TILES))
    def _():
        for ct in range(tn // MXU_COLS):
            store_plain(col_tile(ct), ct)

    @pl.when(j == LAST_TILE)
    def _():
        gate_ct = (COL_GN - LAST_TILE * PROJ_TN) // MXU_COLS
        kv_ct = (COL_KV - LAST_TILE * PROJ_TN) // MXU_COLS
        kv_refs = ((kc0_ref, kc1_ref), (vc0_ref, vc1_ref))
        for ct in range(tn // MXU_COLS):
            a = col_tile(ct)
            if ct < kv_ct:
                if ct == gate_ct:
                    g_ref[...] = jax.nn.sigmoid(a[:, 0:LANES])
                store_rope(a, ct)
            else:
                for g, ref in enumerate(kv_refs[ct - kv_ct]):
                    ref[...] = a[:, g * LANES:g * LANES + HEAD_DIM].astype(BF16)
                store_plain(a, ct)


def _proj(x2, ada3, w, wtv, colscale, cos_t, sa_t, sb_t, seq):
    m, d = x2.shape
    tm = min(1024, seq)
    tn = PROJ_TN
    per_seq = seq // tm
    kv_spec = pl.BlockSpec((tm, HEAD_DIM), lambda i, j: (i, 0))
    kv_shape = jax.ShapeDtypeStruct((m, HEAD_DIM), BF16)
    return pl.pallas_call(
        _proj_kernel,
        grid=(m // tm, w.shape[1] // tn),
        in_specs=[pl.BlockSpec((tm, d), lambda i, j: (i, 0)),
                  pl.BlockSpec((1, 6, d), lambda i, j: (i // per_seq, 0, 0)),
                  pl.BlockSpec((d, tn), lambda i, j: (0, j)),
                  pl.BlockSpec(wtv.shape, lambda i, j: (0, 0)),
                  pl.BlockSpec((1, tn), lambda i, j: (0, j)),
                  pl.BlockSpec((tm, LANES), lambda i, j: (i % per_seq, 0)),
                  pl.BlockSpec((tm, LANES), lambda i, j: (i % per_seq, 0)),
                  pl.BlockSpec((tm, LANES), lambda i, j: (i % per_seq, 0))],
        out_specs=[pl.BlockSpec((tm, tn), lambda i, j: (i, j)),
                   pl.BlockSpec((tm, LANES), lambda i, j: (i, 0)),
                   pl.BlockSpec((tm // VT_CHUNK, ROW_VW, VT_CHUNK), lambda i, j: (i, 0, 0)),
                   pl.BlockSpec((tm // VW_CHUNK, VT_ROWS_ALL - ROW_VW, VW_CHUNK), lambda i, j: (i, 0, 0)),
                   kv_spec, kv_spec, kv_spec, kv_spec],
        out_shape=[jax.ShapeDtypeStruct((m, PROJ_COLS), BF16),
                   jax.ShapeDtypeStruct((m, LANES), F32),
                   jax.ShapeDtypeStruct((m // VT_CHUNK, ROW_VW, VT_CHUNK), BF16),
                   jax.ShapeDtypeStruct((m // VW_CHUNK, VT_ROWS_ALL - ROW_VW, VW_CHUNK), BF16),
                   kv_shape, kv_shape, kv_shape, kv_shape],
        scratch_shapes=[pltpu.VMEM((tm, d), BF16)],
        compiler_params=pltpu.CompilerParams(dimension_semantics=("arbitrary", "arbitrary"),
                                             vmem_limit_bytes=VMEM_LIMIT),
        name="proj",
    )(x2, ada3, w, wtv, colscale, cos_t, sa_t, sb_t)


def _gelu_tanh(x):
    return x * (0.5 * (1.0 + jnp.tanh(math.sqrt(2.0 / math.pi) * (x + 0.044715 * (x * x * x)))))


def _compress_kernel(kc0_ref, kc1_ref, vc0_ref, vc1_ref, w1_ref, pe_ref, w2k_ref, w2vt_ref, k_ref, vt_ref):
    xs = ((kc0_ref, kc1_ref), (vc0_ref, vc1_ref))
    nrow = kc0_ref.shape[1]
    half = w1_ref.shape[1] // 2

    def hidden(which, g):
        w1 = w1_ref[which]
        pe_row = jnp.dot(pe_ref[which], w1, preferred_element_type=F32)[0:1, :]
        x = xs[which][g][0]
        a = jnp.dot(x, w1[0:half], preferred_element_type=F32)
        b = jnp.dot(x, w1[half:2 * half], preferred_element_type=F32)
        return _gelu_tanh(a + pltpu.roll(b, nrow - 1, 0) + pe_row).astype(BF16)

    k_out = jnp.zeros((nrow, LANES), F32)
    vt_out = jnp.zeros((LANES, nrow), F32)
    for g in range(NSA_KV_HEADS):
        k_out = k_out + jnp.dot(hidden(0, g), w2k_ref[g], preferred_element_type=F32)
        vt_out = vt_out + _nt(w2vt_ref[g], hidden(1, g))
    k_ref[0] = k_out.astype(BF16)
    vt_ref[0] = vt_out.astype(BF16)


def _compress(chunks, w1, pe, w2k, w2vt):
    bsz, nrow, width = chunks[0].shape
    const = lambda a: pl.BlockSpec(a.shape, lambda b: (0,) * a.ndim)
    chunk_spec = pl.BlockSpec((1, nrow, width), lambda b: (b, 0, 0))
    return pl.pallas_call(
        _compress_kernel,
        grid=(bsz,),
        in_specs=[chunk_spec] * 4 + [const(w1), const(pe), const(w2k), const(w2vt)],
        out_specs=[pl.BlockSpec((1, nrow, LANES), lambda b: (b, 0, 0)),
                   pl.BlockSpec((1, LANES, nrow), lambda b: (b, 0, 0))],
        out_shape=[jax.ShapeDtypeStruct((bsz, nrow, LANES), BF16),
                   jax.ShapeDtypeStruct((bsz, LANES, nrow), BF16)],
        compiler_params=pltpu.CompilerParams(dimension_semantics=("arbitrary",),
                                             vmem_limit_bytes=VMEM_LIMIT),
        name="compress",
    )(*chunks, w1, pe, w2k, w2vt)


NSA_ROWS = NSA_GROUP * SEL_BLOCK
NSA_PAIR = 4
NSA_LANES = NSA_KV_HEADS * NSA_PAIR * NSA_ROWS
SEL_CHUNK = VT_CHUNK
NSA_TOKENS = NSA_PAIR * SEL_BLOCK
WIN_KEYS = WINDOW + NSA_TOKENS
ONES_ROWS = 16


def _with_ones(vt):
    return jnp.concatenate([vt, jnp.ones((ONES_ROWS, vt.shape[1]), vt.dtype)], axis=0)


def _nsa_kernel(qr_ref, qc_ref, ks_ref, oh_ref, vs_ref, kw_ref, vw_ref, kc_ref, vct_ref, gate_ref,
                ovt_ref, dtab_ref, ctab_ref, o_ref, m_ref, acc_ref, s_ref):
    ip = pl.program_id(1)
    glanes = NSA_PAIR * NSA_ROWS
    assert NSA_TOKENS % VW_CHUNK == 0 and SEL_CHUNK % NSA_TOKENS == 0

    def stack_heads(ref):
        return jnp.concatenate(
            [ref[0, blk * SEL_BLOCK:(blk + 1) * SEL_BLOCK, (g * NSA_GROUP + r) * LANES:(g * NSA_GROUP + r + 1) * LANES]
             for g in range(NSA_KV_HEADS) for blk in range(NSA_PAIR) for r in range(NSA_GROUP)], axis=0)

    def pv(vts, p):
        return jnp.concatenate(
            [jnp.dot(_with_ones(vts[g * HEAD_DIM:(g + 1) * HEAD_DIM]), p[:, g * glanes:(g + 1) * glanes],
                     preferred_element_type=F32) for g in range(NSA_KV_HEADS)], axis=1)

    def block_of(lane_iota, width):
        return NSA_PAIR * ip + ((lane_iota // width) % NSA_PAIR)

    lane128 = lax.broadcasted_iota(jnp.int32, (SEL_BLOCK, LANES), 1)
    tok128 = lax.broadcasted_iota(jnp.int32, (SEL_BLOCK, LANES), 0)
    lower = lane128 < HEAD_DIM
    gates = gate_ref[0, 0]
    i_row = block_of(lax.broadcasted_iota(jnp.int32, (1, NSA_LANES), 1), NSA_ROWS)
    t0_row = i_row * SEL_BLOCK

    qc = stack_heads(qc_ref)
    ctab = ctab_ref[...]
    any_valid = jnp.where(ctab[0:1] <= t0_row, 1.0, 0.0)
    s = jnp.where(ctab <= t0_row, _nt(kc_ref[0], qc), NEG)
    e = jnp.exp(s - jnp.max(s, axis=0, keepdims=True))
    e_hi, e_lo = _split_bf16(e)
    acc = pv(vct_ref[0], e_hi)
    inv = any_valid / acc[HEAD_DIM:HEAD_DIM + 1]
    out = (gates[0:1] * inv) * acc[0:HEAD_DIM]
    imp4 = (jnp.dot(ovt_ref[...], e_hi, preferred_element_type=F32)
            + jnp.dot(ovt_ref[...], e_lo, preferred_element_type=F32)) * inv
    imps = {}
    for g in range(NSA_KV_HEADS):
        for blk in range(NSA_PAIR):
            c0 = (g * NSA_PAIR + blk) * NSA_ROWS
            two = imp4[:, c0:c0 + LANES] + imp4[:, c0 + LANES:c0 + 2 * LANES]
            imps[g, blk] = two + pltpu.roll(two, HEAD_DIM, 1)
    imp_t = jnp.concatenate([jnp.where(lower, imps[0, blk], imps[1, blk]) for blk in range(NSA_PAIR)],
                            axis=1)

    qr = stack_heads(qr_ref)
    nwin = WIN_KEYS // VW_CHUNK
    wb0 = jnp.maximum(ip * (NSA_TOKENS // VW_CHUNK) - WINDOW // VW_CHUNK, 0)
    wrows = pl.ds(pl.multiple_of(wb0 * VW_CHUNK, VW_CHUNK), WIN_KEYS)
    dist = (t0_row - wb0 * VW_CHUNK) - dtab_ref[...]
    s = jnp.where(lax.bitcast_convert_type(dist, jnp.uint32) < WINDOW, _nt(kw_ref[0, wrows, :], qr), NEG)
    e = jnp.exp(s - jnp.max(s, axis=0, keepdims=True)).astype(BF16)
    acc = pv(jnp.concatenate([vw_ref[wb0 + n] for n in range(nwin)], axis=1), e)
    out = out + (gates[2:3] * (1.0 / acc[HEAD_DIM:HEAD_DIM + 1])) * acc[0:HEAD_DIM]

    nb_pad, nsel = imp_t.shape
    jj = lax.broadcasted_iota(jnp.int32, (nb_pad, nsel), 0)
    i_sel = block_of(lax.broadcasted_iota(jnp.int32, (1, nsel), 1), LANES)
    forced = (jj == 0) | (jj == i_sel) | (jj == i_sel - 1)
    val = jnp.where(jj > i_sel, -jnp.inf, jnp.where(forced, FORCE_SCORE, imp_t))
    sub = 8
    blocks = [val[v * sub:(v + 1) * sub, :] for v in range(nb_pad // sub)]
    ranks = [jnp.zeros((sub, nsel), F32) for _ in blocks]
    jloc = lax.broadcasted_iota(jnp.int32, (sub, nsel), 0)
    for k in range(nb_pad):
        rowk = jnp.broadcast_to(val[k:k + 1, :], (sub, nsel))
        for v in range(nb_pad // sub):
            if v * sub > k:
                beats = jnp.where(rowk >= blocks[v], 1.0, 0.0)
            elif (v + 1) * sub <= k:
                beats = jnp.where(rowk > blocks[v], 1.0, 0.0)
            else:
                beats = jnp.where(jloc > k - v * sub,
                                  jnp.where(rowk >= blocks[v], 1.0, 0.0),
                                  jnp.where(rowk > blocks[v], 1.0, 0.0))
            ranks[v] = ranks[v] + beats
    rank = jnp.concatenate(ranks, axis=0)
    sel_bias_t = jnp.where(rank < float(SEL_TOPK), jnp.where(jj <= i_sel, 0.0, NEG), NEG).astype(BF16)
    xg = jnp.concatenate([sel_bias_t, jnp.zeros_like(sel_bias_t)], axis=0)
    lane_sel = lax.broadcasted_iota(jnp.int32, (SEL_BLOCK, nsel), 1)
    tok_sel = lax.broadcasted_iota(jnp.int32, (SEL_BLOCK, nsel), 0)
    bias_rows = []
    for g in range(NSA_KV_HEADS):
        for blk in range(NSA_PAIR):
            pick = jnp.where(lane_sel == tok_sel + blk * LANES + g * SEL_BLOCK, 1.0, 0.0).astype(BF16)
            bias_rows += [_nt(pick, xg).astype(BF16)] * NSA_GROUP
    q_aug = jnp.concatenate([qr, jnp.concatenate(bias_rows, axis=0)], axis=1)

    m_ref[...] = jnp.full(m_ref.shape, NEG, F32)
    acc_ref[...] = jnp.zeros(acc_ref.shape, F32)
    per_chunk = SEL_CHUNK // SEL_BLOCK
    nfull = (NSA_PAIR * ip) // per_chunk

    def scores(c, slot):
        rows = pl.ds(pl.multiple_of(c * SEL_CHUNK, SEL_CHUNK), SEL_CHUNK)
        k_aug = jnp.concatenate([ks_ref[0, rows, :], oh_ref[rows, :]], axis=1)
        s_ref[slot] = _nt(k_aug, q_aug)

    def absorb(c, slot, diag):
        s = s_ref[slot]
        if diag:
            s = jnp.where(dtab_ref[0:SEL_CHUNK, :] > (i_row % per_chunk) * SEL_BLOCK, NEG, s)
        m_old = m_ref[...]
        m_new = jnp.maximum(m_old, jnp.max(s, axis=0, keepdims=True))
        p = jnp.exp(s - m_new).astype(BF16)
        acc_ref[...] = jnp.exp(m_old - m_new) * acc_ref[...] + pv(vs_ref[c], p)
        m_ref[...] = m_new

    scores(0, 0)

    def pair(jp, carry):
        c = 2 * jp
        scores(c + 1, 1)
        absorb(c, 0, False)
        scores(c + 2, 0)
        absorb(c + 1, 1, False)
        return carry

    lax.fori_loop(0, nfull // 2, pair, 0)

    @pl.when(nfull % 2 == 1)
    def _():
        scores(nfull, 1)
        absorb(nfull - 1, 0, False)
        absorb(nfull, 1, True)

    @pl.when(nfull % 2 == 0)
    def _():
        absorb(nfull, 0, True)

    acc = acc_ref[...]
    out = (out + (gates[1:2] * (1.0 / acc[HEAD_DIM:HEAD_DIM + 1])) * acc[0:HEAD_DIM]).astype(BF16)

    fold = jnp.where((lane128 & (SEL_BLOCK - 1)) == tok128, 1.0, 0.0).astype(BF16)
    zero = jnp.zeros((HEAD_DIM, LANES), BF16)
    for blk in range(NSA_PAIR):
        for k in range(NSA_HEADS // 2):
            g, half = divmod(k, NSA_GROUP // 2)
            c0 = (g * NSA_PAIR + blk) * NSA_ROWS + half * LANES
            x = out[:, c0:c0 + LANES]
            y = jnp.concatenate([jnp.where(lower, x, zero), jnp.where(lower, zero, x)], axis=0)
            o_ref[0, blk * SEL_BLOCK:(blk + 1) * SEL_BLOCK, k * LANES:(k + 1) * LANES] = _nt(fold, y).astype(BF16)


def _nsa(proj3, gates_t, onehot, vt512, vw128, kcmp, vct, ovt, dtab, ctab):
    bsz, seq, _ = proj3.shape
    nstep = seq // NSA_TOKENS
    tq = NSA_TOKENS
    qw = NSA_HEADS * LANES
    const = lambda a: pl.BlockSpec(a.shape, lambda b, i: (0,) * a.ndim)
    per_b = lambda a: pl.BlockSpec((1,) + a.shape[1:], lambda b, i: (b,) + (0,) * (a.ndim - 1))
    return pl.pallas_call(
        _nsa_kernel,
        grid=(bsz, nstep),
        in_specs=[pl.BlockSpec((1, tq, qw), lambda b, i: (b, i, COL_QR // qw)),
                  pl.BlockSpec((1, tq, qw), lambda b, i: (b, i, COL_QC // qw)),
                  pl.BlockSpec((1, seq, LANES), lambda b, i: (b, 0, COL_KS // LANES)),
                  const(onehot),
                  pl.BlockSpec((seq // VT_CHUNK, LANES, VT_CHUNK), lambda b, i: (b, ROW_VS // LANES, 0)),
                  pl.BlockSpec((1, seq, LANES), lambda b, i: (b, 0, COL_KW // LANES)),
                  pl.BlockSpec((seq // VW_CHUNK, LANES, VW_CHUNK), lambda b, i: (b, 0, 0)),
                  per_b(kcmp), per_b(vct),
                  pl.BlockSpec((1, 1) + gates_t.shape[2:], lambda b, i: (b, i, 0, 0)),
                  const(ovt), const(dtab), const(ctab)],
        out_specs=pl.BlockSpec((1, tq, NSA_HEADS * HEAD_DIM), lambda b, i: (b, i, 0)),
        out_shape=jax.ShapeDtypeStruct((bsz, seq, NSA_HEADS * HEAD_DIM), BF16),
        scratch_shapes=[pltpu.VMEM((1, NSA_LANES), F32), pltpu.VMEM((HEAD_DIM + ONES_ROWS, NSA_LANES), F32),
                        pltpu.VMEM((2, SEL_CHUNK, NSA_LANES), F32)],
        compiler_params=pltpu.CompilerParams(dimension_semantics=("arbitrary", "arbitrary"),
                                             vmem_limit_bytes=VMEM_LIMIT),
        name="nsa",
    )(proj3, proj3, proj3, onehot, vt512, proj3, vw128, kcmp, vct, gates_t, ovt, dtab, ctab)


DIFF_TQ = VT_CHUNK
DIFF_PAIR = 4


def _diff_kernel(q_ref, k_ref, vt_ref, lam_ref, ng_ref, o_ref, m_ref, acc_ref, s_ref, *, lambda_init):
    i = pl.program_id(2)
    tq = DIFF_TQ
    nv = 2 * HEAD_DIM
    hl = 2 * tq
    lane = lax.broadcasted_iota(jnp.int32, (tq, LANES), 1)
    q_both = []
    for hh in range(DIFF_PAIR):
        q = q_ref[0, :, hh * LANES:(hh + 1) * LANES]
        zero = jnp.zeros_like(q)
        q_both.append(jnp.concatenate([jnp.where(lane < HEAD_DIM, q, zero), jnp.where(lane >= HEAD_DIM, q, zero)],
                                      axis=0))
    m_ref[...] = jnp.full(m_ref.shape, NEG, F32)
    acc_ref[...] = jnp.zeros(acc_ref.shape, F32)

    def scores(c, slot):
        rows = pl.ds(pl.multiple_of(c * tq, tq), tq)
        for hh in range(DIFF_PAIR):
            s_ref[slot, :, hh * hl:(hh + 1) * hl] = _nt(k_ref[0, rows, hh * LANES:(hh + 1) * LANES], q_both[hh])

    def absorb(c, slot, causal):
        s = s_ref[slot]
        if causal:
            kpos = lax.broadcasted_iota(jnp.int32, s.shape, 0)
            qpos = lax.broadcasted_iota(jnp.int32, s.shape, 1) & (tq - 1)
            s = jnp.where(kpos <= qpos, s, NEG)
        m_old = m_ref[...]
        m_new = jnp.maximum(m_old, jnp.max(s, axis=0, keepdims=True))
        p = jnp.exp(s - m_new).astype(BF16)
        vt = vt_ref[c]
        pv = jnp.concatenate([jnp.dot(_with_ones(vt[hh * nv:(hh + 1) * nv]), p[:, hh * hl:(hh + 1) * hl],
                                      preferred_element_type=F32) for hh in range(DIFF_PAIR)], axis=1)
        acc_ref[...] = jnp.exp(m_old - m_new) * acc_ref[...] + pv
        m_ref[...] = m_new

    scores(0, 0)

    def pair(jp, carry):
        c = 2 * jp
        scores(c + 1, 1)
        absorb(c, 0, False)
        scores(c + 2, 0)
        absorb(c + 1, 1, False)
        return carry

    lax.fori_loop(0, i // 2, pair, 0)

    @pl.when(i % 2 == 1)
    def _():
        scores(i, 1)
        absorb(i - 1, 0, False)
        absorb(i, 1, True)

    @pl.when(i % 2 == 0)
    def _():
        absorb(i, 0, True)

    lv = lam_ref[...]
    lam = (jnp.exp(jnp.sum(lv[0:1] * lv[1:2], axis=1, keepdims=True))
           - jnp.exp(jnp.sum(lv[2:3] * lv[3:4], axis=1, keepdims=True)) + lambda_init)
    acc = acc_ref[...]
    on = acc[0:nv] * (1.0 / acc[nv:nv + 1])
    for hh in range(DIFF_PAIR):
        o = on[:, hh * hl:hh * hl + tq] - lam * on[:, hh * hl + tq:(hh + 1) * hl]
        o = o * lax.rsqrt(jnp.mean(o * o, axis=0, keepdims=True) + EPS) * ng_ref[...] * (1.0 - lambda_init)
        o_ref[0, :, hh * LANES:(hh + 1) * LANES] = o.T.astype(BF16)


def _diff(proj3, vt512, lam_vec, norm_g, lambda_init):
    bsz, seq, _ = proj3.shape
    tq = DIFF_TQ
    pw = DIFF_PAIR * LANES
    lanes = DIFF_PAIR * 2 * tq
    kern = functools.partial(_diff_kernel, lambda_init=lambda_init)
    ng = jnp.broadcast_to(norm_g.reshape(LANES, 1), (LANES, tq))
    return pl.pallas_call(
        kern,
        grid=(bsz, DIFF_HEADS // DIFF_PAIR, seq // tq),
        in_specs=[pl.BlockSpec((1, tq, pw), lambda b, h, i: (b, i, COL_QD // pw + h)),
                  pl.BlockSpec((1, seq, pw), lambda b, h, i: (b, 0, COL_KD // pw + h)),
                  pl.BlockSpec((seq // VT_CHUNK, pw, VT_CHUNK), lambda b, h, i: (b, ROW_VD // pw + h, 0)),
                  pl.BlockSpec(lam_vec.shape, lambda b, h, i: (0, 0)),
                  pl.BlockSpec((LANES, tq), lambda b, h, i: (0, 0))],
        out_specs=pl.BlockSpec((1, tq, pw), lambda b, h, i: (b, i, h)),
        out_shape=jax.ShapeDtypeStruct((bsz, seq, DIFF_HEADS * LANES), BF16),
        scratch_shapes=[pltpu.VMEM((1, lanes), F32), pltpu.VMEM((2 * HEAD_DIM + ONES_ROWS, lanes), F32),
                        pltpu.VMEM((2, tq, lanes), F32)],
        compiler_params=pltpu.CompilerParams(dimension_semantics=("arbitrary", "arbitrary", "arbitrary"),
                                             vmem_limit_bytes=VMEM_LIMIT),
        name="diff",
    )(proj3, proj3, vt512, lam_vec, ng)


MERGE_TILE = 256


def _merge_kernel(x_ref, ada_ref, on_ref, od_ref, wg_ref, wbn_ref, wbd_ref, wo_ref, g_ref, b_ref, o_ref, *, alpha):
    x = x_ref[...]
    sh = ada_ref[0, 0:1, :]
    sc = ada_ref[0, 1:2, :]
    gate = ada_ref[0, 2:3, :]
    u = (x * (1.0 + sc) + sh).astype(BF16)
    o_nsa = on_ref[...]
    o_diff = od_ref[...]
    mix = jnp.zeros(x.shape, F32)
    for c in range(x.shape[1] // MERGE_TILE):
        cols = slice(c * MERGE_TILE, (c + 1) * MERGE_TILE)
        gm = jax.nn.sigmoid(jnp.dot(u, wg_ref[:, 2 * c * MERGE_TILE:2 * (c + 1) * MERGE_TILE],
                                    preferred_element_type=F32))
        y_nsa = jnp.dot(o_nsa, wbn_ref[:, cols], preferred_element_type=F32)
        y_diff = jnp.dot(o_diff, wbd_ref[:, cols], preferred_element_type=F32)
        mixed = (gm[:, 0:MERGE_TILE] * y_nsa + gm[:, MERGE_TILE:2 * MERGE_TILE] * y_diff).astype(BF16)
        mix = mix + jnp.dot(mixed, wo_ref[cols, :], preferred_element_type=F32)
    o_ref[...] = _layer_norm(alpha * x + (1.0 + gate) * mix, g_ref[...], b_ref[...])


def _merge(x2, ada3, o_nsa, o_diff, w_gm, w_bn, w_bd, w_o, ln_g, ln_b, seq, alpha):
    m, d = x2.shape
    tm = min(1024, seq)
    per_seq = seq // tm
    w_g2 = jnp.concatenate([w_gm[:, half * d + c * MERGE_TILE:half * d + (c + 1) * MERGE_TILE]
                            for c in range(d // MERGE_TILE) for half in range(2)], axis=1)
    resident = lambda a: pl.BlockSpec(a.shape, lambda i: (0,) * a.ndim, pipeline_mode=pl.Buffered(1))
    kern = functools.partial(_merge_kernel, alpha=alpha)
    return pl.pallas_call(
        kern,
        grid=(m // tm,),
        in_specs=[pl.BlockSpec((tm, d), lambda i: (i, 0)),
                  pl.BlockSpec((1, 6, d), lambda i: (i // per_seq, 0, 0)),
                  pl.BlockSpec((tm, o_nsa.shape[1]), lambda i: (i, 0)),
                  pl.BlockSpec((tm, o_diff.shape[1]), lambda i: (i, 0)),
                  resident(w_g2), resident(w_bn), resident(w_bd), resident(w_o),
                  pl.BlockSpec((1, d), lambda i: (0, 0)),
                  pl.BlockSpec((1, d), lambda i: (0, 0))],
        out_specs=pl.BlockSpec((tm, d), lambda i: (i, 0)),
        out_shape=jax.ShapeDtypeStruct((m, d), F32),
        compiler_params=pltpu.CompilerParams(dimension_semantics=("arbitrary",),
                                             vmem_limit_bytes=VMEM_LIMIT),
        name="merge",
    )(x2, ada3, o_nsa, o_diff, w_g2, w_bn, w_bd, w_o, ln_g.reshape(1, d), ln_b.reshape(1, d))


FFN_TILE = 256


def _ffn_kernel(x_ref, ada_ref, wab_ref, wo_ref, g_ref, b_ref, o_ref, *, alpha):
    x = x_ref[...]
    sh = ada_ref[0, 3:4, :]
    sc = ada_ref[0, 4:5, :]
    gate = ada_ref[0, 5:6, :]
    u = (x * (1.0 + sc) + sh).astype(BF16)
    acc = jnp.zeros(x.shape, F32)
    for c in range(wo_ref.shape[0] // FFN_TILE):
        ab = jnp.dot(u, wab_ref[:, 2 * c * FFN_TILE:2 * (c + 1) * FFN_TILE], preferred_element_type=F32)
        a = ab[:, 0:FFN_TILE]
        h = ((a * jax.nn.sigmoid(a)) * ab[:, FFN_TILE:2 * FFN_TILE]).astype(BF16)
        acc = acc + jnp.dot(h, wo_ref[c * FFN_TILE:(c + 1) * FFN_TILE, :], preferred_element_type=F32)
    o_ref[...] = _layer_norm(alpha * x + (1.0 + gate) * acc, g_ref[...], b_ref[...])


def _ffn(x2, ada3, w_in, w_out, ln_g, ln_b, seq, alpha):
    m, d = x2.shape
    dff = w_out.shape[0]
    tm = min(1024, seq)
    per_seq = seq // tm
    nt = dff // FFN_TILE
    w_ab = jnp.concatenate([w_in[:, half * dff + c * FFN_TILE:half * dff + (c + 1) * FFN_TILE]
                            for c in range(nt) for half in range(2)], axis=1)
    resident = lambda a: pl.BlockSpec(a.shape, lambda i: (0,) * a.ndim, pipeline_mode=pl.Buffered(1))
    kern = functools.partial(_ffn_kernel, alpha=alpha)
    return pl.pallas_call(
        kern,
        grid=(m // tm,),
        in_specs=[pl.BlockSpec((tm, d), lambda i: (i, 0)),
                  pl.BlockSpec((1, 6, d), lambda i: (i // per_seq, 0, 0)),
                  resident(w_ab), resident(w_out),
                  pl.BlockSpec((1, d), lambda i: (0, 0)),
                  pl.BlockSpec((1, d), lambda i: (0, 0))],
        out_specs=pl.BlockSpec((tm, d), lambda i: (i, 0)),
        out_shape=jax.ShapeDtypeStruct((m, d), F32),
        compiler_params=pltpu.CompilerParams(dimension_semantics=("arbitrary",),
                                             vmem_limit_bytes=VMEM_LIMIT),
        name="ffn",
    )(x2, ada3, w_ab, w_out, ln_g.reshape(1, d), ln_b.reshape(1, d))


def _rope_tables(seq):
    half = ROPE_DIMS // 2
    inv_freq = ROPE_THETA ** (-jnp.arange(half, dtype=F32) * 2.0 / ROPE_DIMS)
    ang = jnp.arange(seq, dtype=F32)[:, None] * inv_freq[None, :]
    cos, sin = jnp.cos(ang), jnp.sin(ang)
    ones = jnp.ones((seq, HEAD_DIM - ROPE_DIMS), F32)
    zeros = jnp.zeros((seq, HEAD_DIM - ROPE_DIMS), F32)
    z8 = jnp.zeros((seq, half), F32)
    cos_h = jnp.concatenate([cos, cos, ones], axis=1)
    sa_h = jnp.concatenate([-sin, z8, zeros], axis=1)
    sb_h = jnp.concatenate([z8, sin, zeros], axis=1)
    rep = lambda t: jnp.concatenate([t] * (LANES // HEAD_DIM), axis=1)
    return rep(cos_h), rep(sa_h), rep(sb_h)


def _mask_tables(seq):
    nb = seq // SEL_BLOCK
    nc = (seq - CMP_BLOCK) // CMP_STRIDE + 1
    ncp = seq // CMP_STRIDE
    cs = np.arange(nc) * CMP_STRIDE
    bs = np.arange(nb) * SEL_BLOCK
    ov = np.minimum(cs[:, None] + CMP_BLOCK, bs[None, :] + SEL_BLOCK) - np.maximum(cs[:, None], bs[None, :])
    ov = np.clip(ov, 0, None) / CMP_BLOCK
    ovt = np.zeros((SEL_BLOCK, ncp), np.float32)
    ovt[:nb, :nc] = ov.T
    tl = np.arange(NSA_LANES) % SEL_BLOCK
    dtab = (np.arange(WIN_KEYS)[:, None] - tl[None, :]).astype(np.int32)
    ctab = (np.arange(ncp)[:, None] * CMP_STRIDE + (CMP_BLOCK - 1) - tl[None, :]).astype(np.int32)
    return jnp.asarray(ovt, BF16), jnp.asarray(dtab), jnp.asarray(ctab)


def _proj_weight(w_in_l):
    d = w_in_l.shape[0]
    sizes = (512, 128, 128, 128, 128, 128, 128, 24, 512, 512, 512, 2 * d)
    offs = np.cumsum((0,) + sizes)
    q_n, kc, vc, ks, vs, kw, vw, g_n, q_d, k_d, v_d, g_m = [w_in_l[:, offs[k]:offs[k + 1]] for k in range(12)]
    qh = q_n.reshape(d, NSA_HEADS, HEAD_DIM)
    z = jnp.zeros_like(qh)
    first = jnp.arange(NSA_HEADS)[None, :, None] < NSA_GROUP
    q_wide = jnp.concatenate([jnp.where(first, qh, z), jnp.where(first, z, qh)], axis=-1).reshape(d, NSA_HEADS * LANES)
    zeros = lambda n: jnp.zeros((d, n), w_in_l.dtype)
    half_slabs = [jnp.concatenate([t[:, g * HEAD_DIM:(g + 1) * HEAD_DIM], zeros(LANES - HEAD_DIM)], axis=1)
                  for t in (kc, vc) for g in range(NSA_KV_HEADS)]
    w = jnp.concatenate([q_wide, q_wide, q_d, k_d, ks, kw, g_n, zeros(COL_KV - COL_GN - g_n.shape[1])]
                        + half_slabs, axis=1)
    assert w.shape[1] == PROJ_COLS
    wtv = jnp.concatenate([v_d, vs, vw], axis=1).T
    scale = np.ones((1, PROJ_COLS), np.float32)
    scale[:, COL_QR:COL_QD + 512] = QK_SCALE
    return w.astype(BF16), wtv.astype(BF16), jnp.asarray(scale), g_m.astype(BF16)


def kernel(x, c, w_ada, b_ada, w_in, cmp_pe_k, cmp_w1_k, cmp_w2_k, cmp_pe_v, cmp_w1_v, cmp_w2_v, diff_lambda, diff_norm_g, w_branch_nsa, w_branch_diff, w_out, ln1_g, ln1_b, w_ffn_in, w_ffn_out, ln2_g, ln2_b):
    bsz, seq, d = x.shape
    depth = w_ada.shape[0]
    assert seq % 1024 == 0, seq
    assert seq // SEL_BLOCK <= SEL_BLOCK
    alpha = (2 * depth) ** 0.25
    ncp = seq // CMP_STRIDE
    nblk = seq // SEL_BLOCK
    cos_t, sa_t, sb_t = _rope_tables(seq)
    ovt, dtab, ctab = _mask_tables(seq)
    onehot = (jnp.arange(seq)[:, None] // SEL_BLOCK == jnp.arange(LANES)[None, :]).astype(BF16)

    x2 = x.reshape(bsz * seq, d)
    for l in range(depth):
        lambda_init = 0.8 - 0.6 * math.exp(-0.3 * l)
        ada3 = _ada(c, w_ada[l], b_ada[l]).reshape(bsz, 6, d)
        w_proj, wtv, colscale, w_gm = _proj_weight(w_in[l])
        proj, gates, vt512, vw128, *kv_raw = _proj(x2, ada3, w_proj, wtv, colscale, cos_t, sa_t, sb_t, seq)
        proj3 = proj.reshape(bsz, seq, PROJ_COLS)
        chunks = [t.reshape(bsz, ncp, CMP_STRIDE * HEAD_DIM) for t in kv_raw]
        w1 = jnp.stack([cmp_w1_k[l], cmp_w1_v[l]]).astype(BF16)
        pe = jnp.stack([cmp_pe_k[l], cmp_pe_v[l]]).reshape(2, 1, CMP_BLOCK * HEAD_DIM)
        pe = jnp.broadcast_to(pe, (2, 8, CMP_BLOCK * HEAD_DIM)).astype(BF16)
        w2k = cmp_w2_k[l].astype(BF16)
        w2v = cmp_w2_v[l].astype(BF16).T
        w2k = jnp.stack([jnp.concatenate([w2k, jnp.zeros_like(w2k)], axis=1),
                         jnp.concatenate([jnp.zeros_like(w2k), w2k], axis=1)])
        w2vt = jnp.stack([jnp.concatenate([w2v, jnp.zeros_like(w2v)], axis=0),
                          jnp.concatenate([jnp.zeros_like(w2v), w2v], axis=0)])
        kcmp, vct = _compress(chunks, w1, pe, w2k, w2vt)

        nstep = nblk // NSA_PAIR
        gates_t = gates.reshape(bsz, nstep, NSA_PAIR, SEL_BLOCK, LANES)[..., :NSA_HEADS * 3]
        gates_t = gates_t.reshape(bsz, nstep, NSA_PAIR, SEL_BLOCK, NSA_KV_HEADS, NSA_GROUP, 3)
        gates_t = gates_t.transpose(0, 1, 6, 4, 2, 5, 3).reshape(bsz, nstep, 3, NSA_LANES)
        gates_t = jnp.pad(gates_t, ((0, 0), (0, 0), (0, 8 - 3), (0, 0)))
        o_nsa = _nsa(proj3, gates_t, onehot, vt512, vw128, kcmp, vct, ovt, dtab, ctab)
        o_diff = _diff(proj3, vt512, diff_lambda[l], diff_norm_g[l], lambda_init)

        x2 = _merge(x2, ada3, o_nsa.reshape(bsz * seq, -1), o_diff.reshape(bsz * seq, -1), w_gm,
                    w_branch_nsa[l].astype(BF16), w_branch_diff[l].astype(BF16), w_out[l].astype(BF16),
                    ln1_g[l], ln1_b[l], seq, alpha)
        x2 = _ffn(x2, ada3, w_ffn_in[l].astype(BF16), w_ffn_out[l].astype(BF16), ln2_g[l], ln2_b[l], seq, alpha)
    return x2.reshape(bsz, seq, d)
```

```python
import functools
import math

import numpy as np
import jax
import jax.numpy as jnp
from jax import lax
from jax.experimental import pallas as pl
from jax.experimental.pallas import tpu as pltpu

F32 = jnp.float32
BF16 = jnp.bfloat16

HEAD_DIM = 64
NSA_HEADS = 8
NSA_KV_HEADS = 2
NSA_GROUP = NSA_HEADS // NSA_KV_HEADS
CMP_BLOCK = 32
CMP_STRIDE = 16
CMP_HIDDEN = 256
SEL_BLOCK = 64
SEL_TOPK = 16
WINDOW = 512
DIFF_HEADS = 4
ROPE_THETA = 500000.0
ROPE_DIMS = HEAD_DIM // 4
EPS = 1e-5
NEG = -1e30
FORCE_SCORE = 1e9
QK_SCALE = HEAD_DIM ** -0.5 * math.log2(math.e)

LANES = 128
VMEM_LIMIT = 56 * 1024 * 1024

PROJ_TN = 1024
MXU_COLS = 256
COL_QR = 0
COL_QC = 1024
COL_QD = 2048
COL_KD = 2560
COL_KS = 3072
COL_KW = 3200
COL_GN = 3328
COL_KV = 3584
PROJ_COLS = 4096
ROPE_TILES = (0, 2)
PLAIN_TILES = (1,)
LAST_TILE = 3
ROW_VD = 0
ROW_VS = 512
ROW_VW = 640
VT_ROWS_ALL = 768
VT_CHUNK = 512
VW_CHUNK = 128


def _nt(a, b):
    return lax.dot_general(a, b, (((1,), (1,)), ((), ())), preferred_element_type=F32)


def _split_bf16(a):
    hi = a.astype(BF16)
    lo = (a - hi.astype(F32)).astype(BF16)
    return hi, lo


def _layer_norm(h, g, b):
    mu = jnp.mean(h, axis=-1, keepdims=True)
    d = h - mu
    var = jnp.mean(d * d, axis=-1, keepdims=True)
    return d * lax.rsqrt(var + EPS) * g + b


def _ada_kernel(c_ref, w_ref, b_ref, o_ref):
    c = c_ref[...]
    a = c * jax.nn.sigmoid(c)
    a_hi, a_lo = _split_bf16(a)
    w_hi, w_lo = _split_bf16(w_ref[...])
    acc = jnp.dot(a_hi, w_hi, preferred_element_type=F32)
    acc += jnp.dot(a_lo, w_hi, preferred_element_type=F32)
    acc += jnp.dot(a_hi, w_lo, preferred_element_type=F32)
    o_ref[...] = acc + b_ref[...]


def _ada(c, w, b):
    bsz, d = c.shape
    n = w.shape[1]
    tn = 1024
    return pl.pallas_call(
        _ada_kernel,
        grid=(n // tn,),
        in_specs=[pl.BlockSpec((bsz, d), lambda j: (0, 0)),
                  pl.BlockSpec((d, tn), lambda j: (0, j)),
                  pl.BlockSpec((1, tn), lambda j: (0, j))],
        out_specs=pl.BlockSpec((bsz, tn), lambda j: (0, j)),
        out_shape=jax.ShapeDtypeStruct((bsz, n), F32),
        compiler_params=pltpu.CompilerParams(dimension_semantics=("arbitrary",),
                                             vmem_limit_bytes=VMEM_LIMIT),
        name="ada",
    )(c, w, b.reshape(1, n))


def _proj_kernel(x_ref, ada_ref, w_ref, wtv_ref, cs_ref, cos_ref, sa_ref, sb_ref,
                 o_ref, g_ref, vt_ref, vw_ref, kc0_ref, kc1_ref, vc0_ref, vc1_ref, u_ref):
    j = pl.program_id(1)
    tm = u_ref.shape[0]
    tn = w_ref.shape[1]

    @pl.when(j == 0)
    def _():
        sh = ada_ref[0, 0:1, :]
        sc = ada_ref[0, 1:2, :]
        u_ref[...] = (x_ref[...] * (1.0 + sc) + sh).astype(BF16)
        vt = _nt(wtv_ref[...], u_ref[...]).astype(BF16)
        for cc in range(tm // VT_CHUNK):
            vt_ref[cc] = vt[ROW_VD:ROW_VW, cc * VT_CHUNK:(cc + 1) * VT_CHUNK]
        for cc in range(tm // VW_CHUNK):
            vw_ref[cc] = vt[ROW_VW:VT_ROWS_ALL, cc * VW_CHUNK:(cc + 1) * VW_CHUNK]

    def col_tile(ct):
        cols = slice(ct * MXU_COLS, (ct + 1) * MXU_COLS)
        return jnp.dot(u_ref[...], w_ref[:, cols], preferred_element_type=F32) * cs_ref[:, cols]

    def store_rope(a, ct):
        cosv, sav, sbv = cos_ref[...], sa_ref[...], sb_ref[...]
        for s in range(MXU_COLS // LANES):
            a_s = a[:, s * LANES:(s + 1) * LANES]
            r = a_s * cosv + pltpu.roll(a_s, LANES - ROPE_DIMS // 2, 1) * sav + pltpu.roll(a_s, ROPE_DIMS // 2, 1) * sbv
            o_ref[:, ct * MXU_COLS + s * LANES:ct * MXU_COLS + (s + 1) * LANES] = r.astype(BF16)

    def store_plain(a, ct):
        o_ref[:, ct * MXU_COLS:(ct + 1) * MXU_COLS] = a.astype(BF16)

    def any_of(tiles):
        return functools.reduce(jnp.logical_or, [j == t for t in tiles])

    @pl.when(any_of(ROPE_TILES))
    def _():
        for ct in range(tn // MXU_COLS):
            store_rope(col_tile(ct), ct)

    @pl.when(any_of(PLAIN_TILES))
    def _():
        for ct in range(tn // MXU_COLS):
            store_plain(col_tile(ct), ct)

    @pl.when(j == LAST_TILE)
    def _():
        gate_ct = (COL_GN - LAST_TILE * PROJ_TN) // MXU_COLS
        kv_ct = (COL_KV - LAST_TILE * PROJ_TN) // MXU_COLS
        kv_refs = ((kc0_ref, kc1_ref), (vc0_ref, vc1_ref))
        for ct in range(tn // MXU_COLS):
            a = col_tile(ct)
            if ct < kv_ct:
                if ct == gate_ct:
                    g_ref[...] = jax.nn.sigmoid(a[:, 0:LANES])
                store_rope(a, ct)
            else:
                for g, ref in enumerate(kv_refs[ct - kv_ct]):
                    ref[...] = a[:, g * LANES:g * LANES + HEAD_DIM].astype(BF16)
                store_plain(a, ct)


def _proj(x2, ada3, w, wtv, colscale, cos_t, sa_t, sb_t, seq):
    m, d = x2.shape
    tm = min(1024, seq)
    tn = PROJ_TN
    per_seq = seq // tm
    kv_spec = pl.BlockSpec((tm, HEAD_DIM), lambda i, j: (i, 0))
    kv_shape = jax.ShapeDtypeStruct((m, HEAD_DIM), BF16)
    return pl.pallas_call(
        _proj_kernel,
        grid=(m // tm, w.shape[1] // tn),
        in_specs=[pl.BlockSpec((tm, d), lambda i, j: (i, 0)),
                  pl.BlockSpec((1, 6, d), lambda i, j: (i // per_seq, 0, 0)),
                  pl.BlockSpec((d, tn), lambda i, j: (0, j)),
                  pl.BlockSpec(wtv.shape, lambda i, j: (0, 0)),
                  pl.BlockSpec((1, tn), lambda i, j: (0, j)),
                  pl.BlockSpec((tm, LANES), lambda i, j: (i % per_seq, 0)),
                  pl.BlockSpec((tm, LANES), lambda i, j: (i % per_seq, 0)),
                  pl.BlockSpec((tm, LANES), lambda i, j: (i % per_seq, 0))],
        out_specs=[pl.BlockSpec((tm, tn), lambda i, j: (i, j)),
                   pl.BlockSpec((tm, LANES), lambda i, j: (i, 0)),
                   pl.BlockSpec((tm // VT_CHUNK, ROW_VW, VT_CHUNK), lambda i, j: (i, 0, 0)),
                   pl.BlockSpec((tm // VW_CHUNK, VT_ROWS_ALL - ROW_VW, VW_CHUNK), lambda i, j: (i, 0, 0)),
                   kv_spec, kv_spec, kv_spec, kv_spec],
        out_shape=[jax.ShapeDtypeStruct((m, PROJ_COLS), BF16),
                   jax.ShapeDtypeStruct((m, LANES), F32),
                   jax.ShapeDtypeStruct((m // VT_CHUNK, ROW_VW, VT_CHUNK), BF16),
                   jax.ShapeDtypeStruct((m // VW_CHUNK, VT_ROWS_ALL - ROW_VW, VW_CHUNK), BF16),
                   kv_shape, kv_shape, kv_shape, kv_shape],
        scratch_shapes=[pltpu.VMEM((tm, d), BF16)],
        compiler_params=pltpu.CompilerParams(dimension_semantics=("arbitrary", "arbitrary"),
                                             vmem_limit_bytes=VMEM_LIMIT),
        name="proj",
    )(x2, ada3, w, wtv, colscale, cos_t, sa_t, sb_t)


def _gelu_tanh(x):
    return x * (0.5 * (1.0 + jnp.tanh(math.sqrt(2.0 / math.pi) * (x + 0.044715 * (x * x * x)))))


def _compress_kernel(kc0_ref, kc1_ref, vc0_ref, vc1_ref, w1_ref, pe_ref, w2k_ref, w2vt_ref, k_ref, vt_ref):
    xs = ((kc0_ref, kc1_ref), (vc0_ref, vc1_ref))
    nrow = kc0_ref.shape[1]
    half = w1_ref.shape[1] // 2

    def hidden(which, g):
        w1 = w1_ref[which]
        pe_row = jnp.dot(pe_ref[which], w1, preferred_element_type=F32)[0:1, :]
        x = xs[which][g][0]
        a = jnp.dot(x, w1[0:half], preferred_element_type=F32)
        b = jnp.dot(x, w1[half:2 * half], preferred_element_type=F32)
        return _gelu_tanh(a + pltpu.roll(b, nrow - 1, 0) + pe_row).astype(BF16)

    k_out = jnp.zeros((nrow, LANES), F32)
    vt_out = jnp.zeros((LANES, nrow), F32)
    for g in range(NSA_KV_HEADS):
        k_out = k_out + jnp.dot(hidden(0, g), w2k_ref[g], preferred_element_type=F32)
        vt_out = vt_out + _nt(w2vt_ref[g], hidden(1, g))
    k_ref[0] = k_out.astype(BF16)
    vt_ref[0] = vt_out.astype(BF16)


def _compress(chunks, w1, pe, w2k, w2vt):
    bsz, nrow, width = chunks[0].shape
    const = lambda a: pl.BlockSpec(a.shape, lambda b: (0,) * a.ndim)
    chunk_spec = pl.BlockSpec((1, nrow, width), lambda b: (b, 0, 0))
    return pl.pallas_call(
        _compress_kernel,
        grid=(bsz,),
        in_specs=[chunk_spec] * 4 + [const(w1), const(pe), const(w2k), const(w2vt)],
        out_specs=[pl.BlockSpec((1, nrow, LANES), lambda b: (b, 0, 0)),
                   pl.BlockSpec((1, LANES, nrow), lambda b: (b, 0, 0))],
        out_shape=[jax.ShapeDtypeStruct((bsz, nrow, LANES), BF16),
                   jax.ShapeDtypeStruct((bsz, LANES, nrow), BF16)],
        compiler_params=pltpu.CompilerParams(dimension_semantics=("arbitrary",),
                                             vmem_limit_bytes=VMEM_LIMIT),
        name="compress",
    )(*chunks, w1, pe, w2k, w2vt)


NSA_ROWS = NSA_GROUP * SEL_BLOCK
NSA_PAIR = 4
NSA_LANES = NSA_KV_HEADS * NSA_PAIR * NSA_ROWS
SEL_CHUNK = VT_CHUNK
NSA_TOKENS = NSA_PAIR * SEL_BLOCK
WIN_KEYS = WINDOW + NSA_TOKENS
ONES_ROWS = 16


def _with_ones(vt):
    return jnp.concatenate([vt, jnp.ones((ONES_ROWS, vt.shape[1]), vt.dtype)], axis=0)


def _nsa_kernel(qr_ref, qc_ref, ks_ref, oh_ref, vs_ref, kw_ref, vw_ref, kc_ref, vct_ref, gate_ref,
                ovt_ref, dtab_ref, ctab_ref, o_ref, m_ref, acc_ref, s_ref):
    ip = pl.program_id(1)
    glanes = NSA_PAIR * NSA_ROWS
    assert NSA_TOKENS % VW_CHUNK == 0 and SEL_CHUNK % NSA_TOKENS == 0

    def stack_heads(ref):
        return jnp.concatenate(
            [ref[0, blk * SEL_BLOCK:(blk + 1) * SEL_BLOCK, (g * NSA_GROUP + r) * LANES:(g * NSA_GROUP + r + 1) * LANES]
             for g in range(NSA_KV_HEADS) for blk in range(NSA_PAIR) for r in range(NSA_GROUP)], axis=0)

    def pv(vts, p):
        return jnp.concatenate(
            [jnp.dot(_with_ones(vts[g * HEAD_DIM:(g + 1) * HEAD_DIM]), p[:, g * glanes:(g + 1) * glanes],
                     preferred_element_type=F32) for g in range(NSA_KV_HEADS)], axis=1)

    def block_of(lane_iota, width):
        return NSA_PAIR * ip + ((lane_iota // width) % NSA_PAIR)

    lane128 = lax.broadcasted_iota(jnp.int32, (SEL_BLOCK, LANES), 1)
    tok128 = lax.broadcasted_iota(jnp.int32, (SEL_BLOCK, LANES), 0)
    lower = lane128 < HEAD_DIM
    gates = gate_ref[0, 0]
    i_row = block_of(lax.broadcasted_iota(jnp.int32, (1, NSA_LANES), 1), NSA_ROWS)
    t0_row = i_row * SEL_BLOCK

    qc = stack_heads(qc_ref)
    ctab = ctab_ref[...]
    any_valid = jnp.where(ctab[0:1] <= t0_row, 1.0, 0.0)
    s = jnp.where(ctab <= t0_row, _nt(kc_ref[0], qc), NEG)
    e = jnp.exp2(s - jnp.max(s, axis=0, keepdims=True))
    e_hi, e_lo = _split_bf16(e)
    acc = pv(vct_ref[0], e_hi)
    inv = any_valid / acc[HEAD_DIM:HEAD_DIM + 1]
    out = (gates[0:1] * inv) * acc[0:HEAD_DIM]
    imp4 = (jnp.dot(ovt_ref[...], e_hi, preferred_element_type=F32)
            + jnp.dot(ovt_ref[...], e_lo, preferred_element_type=F32)) * inv
    imps = {}
    for g in range(NSA_KV_HEADS):
        for blk in range(NSA_PAIR):
            c0 = (g * NSA_PAIR + blk) * NSA_ROWS
            two = imp4[:, c0:c0 + LANES] + imp4[:, c0 + LANES:c0 + 2 * LANES]
            imps[g, blk] = two + pltpu.roll(two, HEAD_DIM, 1)
    imp_t = jnp.concatenate([jnp.where(lower, imps[0, blk], imps[1, blk]) for blk in range(NSA_PAIR)],
                            axis=1)

    qr = stack_heads(qr_ref)
    nwin = WIN_KEYS // VW_CHUNK
    wb0 = jnp.maximum(ip * (NSA_TOKENS // VW_CHUNK) - WINDOW // VW_CHUNK, 0)
    wrows = pl.ds(pl.multiple_of(wb0 * VW_CHUNK, VW_CHUNK), WIN_KEYS)
    dist = (t0_row - wb0 * VW_CHUNK) - dtab_ref[...]
    s = jnp.where(lax.bitcast_convert_type(dist, jnp.uint32) < WINDOW, _nt(kw_ref[0, wrows, :], qr), NEG)
    e = jnp.exp2(s - jnp.max(s, axis=0, keepdims=True)).astype(BF16)
    acc = pv(jnp.concatenate([vw_ref[wb0 + n] for n in range(nwin)], axis=1), e)
    out = out + (gates[2:3] * (1.0 / acc[HEAD_DIM:HEAD_DIM + 1])) * acc[0:HEAD_DIM]

    nb_pad, nsel = imp_t.shape
    jj = lax.broadcasted_iota(jnp.int32, (nb_pad, nsel), 0)
    i_sel = block_of(lax.broadcasted_iota(jnp.int32, (1, nsel), 1), LANES)
    forced = (jj == 0) | (jj == i_sel) | (jj == i_sel - 1)
    val = jnp.where(jj > i_sel, -jnp.inf, jnp.where(forced, FORCE_SCORE, imp_t))
    sub = 8
    blocks = [val[v * sub:(v + 1) * sub, :] for v in range(nb_pad // sub)]
    ranks = [jnp.zeros((sub, nsel), F32) for _ in blocks]
    jloc = lax.broadcasted_iota(jnp.int32, (sub, nsel), 0)
    for k in range(nb_pad):
        rowk = jnp.broadcast_to(val[k:k + 1, :], (sub, nsel))
        for v in range(nb_pad // sub):
            if v * sub > k:
                beats = jnp.where(rowk >= blocks[v], 1.0, 0.0)
            elif (v + 1) * sub <= k:
                beats = jnp.where(rowk > blocks[v], 1.0, 0.0)
            else:
                beats = jnp.where(jloc > k - v * sub,
                                  jnp.where(rowk >= blocks[v], 1.0, 0.0),
                                  jnp.where(rowk > blocks[v], 1.0, 0.0))
            ranks[v] = ranks[v] + beats
    rank = jnp.concatenate(ranks, axis=0)
    sel_bias_t = jnp.where(rank < float(SEL_TOPK), jnp.where(jj <= i_sel, 0.0, NEG), NEG).astype(BF16)
    xg = jnp.concatenate([sel_bias_t, jnp.zeros_like(sel_bias_t)], axis=0)
    lane_sel = lax.broadcasted_iota(jnp.int32, (SEL_BLOCK, nsel), 1)
    tok_sel = lax.broadcasted_iota(jnp.int32, (SEL_BLOCK, nsel), 0)
    bias_rows = []
    for g in range(NSA_KV_HEADS):
        for blk in range(NSA_PAIR):
            pick = jnp.where(lane_sel == tok_sel + blk * LANES + g * SEL_BLOCK, 1.0, 0.0).astype(BF16)
            bias_rows += [_nt(pick, xg).astype(BF16)] * NSA_GROUP
    q_aug = jnp.concatenate([qr, jnp.concatenate(bias_rows, axis=0)], axis=1)

    m_ref[...] = jnp.full(m_ref.shape, NEG, F32)
    acc_ref[...] = jnp.zeros(acc_ref.shape, F32)
    per_chunk = SEL_CHUNK // SEL_BLOCK
    nfull = (NSA_PAIR * ip) // per_chunk

    def scores(c, slot):
        rows = pl.ds(pl.multiple_of(c * SEL_CHUNK, SEL_CHUNK), SEL_CHUNK)
        k_aug = jnp.concatenate([ks_ref[0, rows, :], oh_ref[rows, :]], axis=1)
        s_ref[slot] = _nt(k_aug, q_aug)

    def absorb(c, slot, diag):
        s = s_ref[slot]
        if diag:
            s = jnp.where(dtab_ref[0:SEL_CHUNK, :] > (i_row % per_chunk) * SEL_BLOCK, NEG, s)
        m_old = m_ref[...]
        m_new = jnp.maximum(m_old, jnp.max(s, axis=0, keepdims=True))
        p = jnp.exp2(s - m_new).astype(BF16)
        acc_ref[...] = jnp.exp2(m_old - m_new) * acc_ref[...] + pv(vs_ref[c], p)
        m_ref[...] = m_new

    scores(0, 0)

    def pair(jp, carry):
        c = 2 * jp
        scores(c + 1, 1)
        absorb(c, 0, False)
        scores(c + 2, 0)
        absorb(c + 1, 1, False)
        return carry

    lax.fori_loop(0, nfull // 2, pair, 0)

    @pl.when(nfull % 2 == 1)
    def _():
        scores(nfull, 1)
        absorb(nfull - 1, 0, False)
        absorb(nfull, 1, True)

    @pl.when(nfull % 2 == 0)
    def _():
        absorb(nfull, 0, True)

    acc = acc_ref[...]
    out = (out + (gates[1:2] * (1.0 / acc[HEAD_DIM:HEAD_DIM + 1])) * acc[0:HEAD_DIM]).astype(BF16)

    fold = jnp.where((lane128 & (SEL_BLOCK - 1)) == tok128, 1.0, 0.0).astype(BF16)
    zero = jnp.zeros((HEAD_DIM, LANES), BF16)
    for blk in range(NSA_PAIR):
        for k in range(NSA_HEADS // 2):
            g, half = divmod(k, NSA_GROUP // 2)
            c0 = (g * NSA_PAIR + blk) * NSA_ROWS + half * LANES
            x = out[:, c0:c0 + LANES]
            y = jnp.concatenate([jnp.where(lower, x, zero), jnp.where(lower, zero, x)], axis=0)
            o_ref[0, blk * SEL_BLOCK:(blk + 1) * SEL_BLOCK, k * LANES:(k + 1) * LANES] = _nt(fold, y).astype(BF16)


def _nsa(proj3, gates_t, onehot, vt512, vw128, kcmp, vct, ovt, dtab, ctab):
    bsz, seq, _ = proj3.shape
    nstep = seq // NSA_TOKENS
    tq = NSA_TOKENS
    qw = NSA_HEADS * LANES
    const = lambda a: pl.BlockSpec(a.shape, lambda b, i: (0,) * a.ndim)
    per_b = lambda a: pl.BlockSpec((1,) + a.shape[1:], lambda b, i: (b,) + (0,) * (a.ndim - 1))
    return pl.pallas_call(
        _nsa_kernel,
        grid=(bsz, nstep),
        in_specs=[pl.BlockSpec((1, tq, qw), lambda b, i: (b, i, COL_QR // qw)),
                  pl.BlockSpec((1, tq, qw), lambda b, i: (b, i, COL_QC // qw)),
                  pl.BlockSpec((1, seq, LANES), lambda b, i: (b, 0, COL_KS // LANES)),
                  const(onehot),
                  pl.BlockSpec((seq // VT_CHUNK, LANES, VT_CHUNK), lambda b, i: (b, ROW_VS // LANES, 0)),
                  pl.BlockSpec((1, seq, LANES), lambda b, i: (b, 0, COL_KW // LANES)),
                  pl.BlockSpec((seq // VW_CHUNK, LANES, VW_CHUNK), lambda b, i: (b, 0, 0)),
                  per_b(kcmp), per_b(vct),
                  pl.BlockSpec((1, 1) + gates_t.shape[2:], lambda b, i: (b, i, 0, 0)),
                  const(ovt), const(dtab), const(ctab)],
        out_specs=pl.BlockSpec((1, tq, NSA_HEADS * HEAD_DIM), lambda b, i: (b, i, 0)),
        out_shape=jax.ShapeDtypeStruct((bsz, seq, NSA_HEADS * HEAD_DIM), BF16),
        scratch_shapes=[pltpu.VMEM((1, NSA_LANES), F32), pltpu.VMEM((HEAD_DIM + ONES_ROWS, NSA_LANES), F32),
                        pltpu.VMEM((2, SEL_CHUNK, NSA_LANES), F32)],
        compiler_params=pltpu.CompilerParams(dimension_semantics=("arbitrary", "arbitrary"),
                                             vmem_limit_bytes=VMEM_LIMIT),
        name="nsa",
    )(proj3, proj3, proj3, onehot, vt512, proj3, vw128, kcmp, vct, gates_t, ovt, dtab, ctab)


DIFF_TQ = VT_CHUNK
DIFF_PAIR = 4


def _diff_kernel(q_ref, k_ref, vt_ref, lam_ref, ng_ref, o_ref, m_ref, acc_ref, s_ref, *, lambda_init):
    i = pl.program_id(2)
    tq = DIFF_TQ
    nv = 2 * HEAD_DIM
    hl = 2 * tq
    lane = lax.broadcasted_iota(jnp.int32, (tq, LANES), 1)
    q_both = []
    for hh in range(DIFF_PAIR):
        q = q_ref[0, :, hh * LANES:(hh + 1) * LANES]
        zero = jnp.zeros_like(q)
        q_both.append(jnp.concatenate([jnp.where(lane < HEAD_DIM, q, zero), jnp.where(lane >= HEAD_DIM, q, zero)],
                                      axis=0))
    m_ref[...] = jnp.full(m_ref.shape, NEG, F32)
    acc_ref[...] = jnp.zeros(acc_ref.shape, F32)

    def scores(c, slot):
        rows = pl.ds(pl.multiple_of(c * tq, tq), tq)
        for hh in range(DIFF_PAIR):
            s_ref[slot, :, hh * hl:(hh + 1) * hl] = _nt(k_ref[0, rows, hh * LANES:(hh + 1) * LANES], q_both[hh])

    def absorb(c, slot, causal):
        s = s_ref[slot]
        if causal:
            kpos = lax.broadcasted_iota(jnp.int32, s.shape, 0)
            qpos = lax.broadcasted_iota(jnp.int32, s.shape, 1) & (tq - 1)
            s = jnp.where(kpos <= qpos, s, NEG)
        m_old = m_ref[...]
        m_new = jnp.maximum(m_old, jnp.max(s, axis=0, keepdims=True))
        p = jnp.exp2(s - m_new).astype(BF16)
        vt = vt_ref[c]
        pv = jnp.concatenate([jnp.dot(_with_ones(vt[hh * nv:(hh + 1) * nv]), p[:, hh * hl:(hh + 1) * hl],
                                      preferred_element_type=F32) for hh in range(DIFF_PAIR)], axis=1)
        acc_ref[...] = jnp.exp2(m_old - m_new) * acc_ref[...] + pv
        m_ref[...] = m_new

    scores(0, 0)

    def pair(jp, carry):
        c = 2 * jp
        scores(c + 1, 1)
        absorb(c, 0, False)
        scores(c + 2, 0)
        absorb(c + 1, 1, False)
        return carry

    lax.fori_loop(0, i // 2, pair, 0)

    @pl.when(i % 2 == 1)
    def _():
        scores(i, 1)
        absorb(i - 1, 0, False)
        absorb(i, 1, True)

    @pl.when(i % 2 == 0)
    def _():
        absorb(i, 0, True)

    lv = lam_ref[...]
    lam = (jnp.exp(jnp.sum(lv[0:1] * lv[1:2], axis=1, keepdims=True))
           - jnp.exp(jnp.sum(lv[2:3] * lv[3:4], axis=1, keepdims=True)) + lambda_init)
    acc = acc_ref[...]
    on = acc[0:nv] * (1.0 / acc[nv:nv + 1])
    for hh in range(DIFF_PAIR):
        o = on[:, hh * hl:hh * hl + tq] - lam * on[:, hh * hl + tq:(hh + 1) * hl]
        o = o * lax.rsqrt(jnp.mean(o * o, axis=0, keepdims=True) + EPS) * ng_ref[...] * (1.0 - lambda_init)
        o_ref[0, :, hh * LANES:(hh + 1) * LANES] = o.T.astype(BF16)


def _diff(proj3, vt512, lam_vec, norm_g, lambda_init):
    bsz, seq, _ = proj3.shape
    tq = DIFF_TQ
    pw = DIFF_PAIR * LANES
    lanes = DIFF_PAIR * 2 * tq
    kern = functools.partial(_diff_kernel, lambda_init=lambda_init)
    ng = jnp.broadcast_to(norm_g.reshape(LANES, 1), (LANES, tq))
    return pl.pallas_call(
        kern,
        grid=(bsz, DIFF_HEADS // DIFF_PAIR, seq // tq),
        in_specs=[pl.BlockSpec((1, tq, pw), lambda b, h, i: (b, i, COL_QD // pw + h)),
                  pl.BlockSpec((1, seq, pw), lambda b, h, i: (b, 0, COL_KD // pw + h)),
                  pl.BlockSpec((seq // VT_CHUNK, pw, VT_CHUNK), lambda b, h, i: (b, ROW_VD // pw + h, 0)),
                  pl.BlockSpec(lam_vec.shape, lambda b, h, i: (0, 0)),
                  pl.BlockSpec((LANES, tq), lambda b, h, i: (0, 0))],
        out_specs=pl.BlockSpec((1, tq, pw), lambda b, h, i: (b, i, h)),
        out_shape=jax.ShapeDtypeStruct((bsz, seq, DIFF_HEADS * LANES), BF16),
        scratch_shapes=[pltpu.VMEM((1, lanes), F32), pltpu.VMEM((2 * HEAD_DIM + ONES_ROWS, lanes), F32),
                        pltpu.VMEM((2, tq, lanes), F32)],
        compiler_params=pltpu.CompilerParams(dimension_semantics=("arbitrary", "arbitrary", "arbitrary"),
                                             vmem_limit_bytes=VMEM_LIMIT),
        name="diff",
    )(proj3, proj3, vt512, lam_vec, ng)


MERGE_TILE = 256


def _merge_kernel(x_ref, ada_ref, on_ref, od_ref, wg_ref, wbn_ref, wbd_ref, wo_ref, g_ref, b_ref, o_ref, *, alpha):
    x = x_ref[...]
    sh = ada_ref[0, 0:1, :]
    sc = ada_ref[0, 1:2, :]
    gate = ada_ref[0, 2:3, :]
    u = (x * (1.0 + sc) + sh).astype(BF16)
    o_nsa = on_ref[...]
    o_diff = od_ref[...]
    mix = jnp.zeros(x.shape, F32)
    for c in range(x.shape[1] // MERGE_TILE):
        cols = slice(c * MERGE_TILE, (c + 1) * MERGE_TILE)
        gm = jax.nn.sigmoid(jnp.dot(u, wg_ref[:, 2 * c * MERGE_TILE:2 * (c + 1) * MERGE_TILE],
                                    preferred_element_type=F32))
        y_nsa = jnp.dot(o_nsa, wbn_ref[:, cols], preferred_element_type=F32)
        y_diff = jnp.dot(o_diff, wbd_ref[:, cols], preferred_element_type=F32)
        mixed = (gm[:, 0:MERGE_TILE] * y_nsa + gm[:, MERGE_TILE:2 * MERGE_TILE] * y_diff).astype(BF16)
        mix = mix + jnp.dot(mixed, wo_ref[cols, :], preferred_element_type=F32)
    o_ref[...] = _layer_norm(alpha * x + (1.0 + gate) * mix, g_ref[...], b_ref[...])


def _merge(x2, ada3, o_nsa, o_diff, w_gm, w_bn, w_bd, w_o, ln_g, ln_b, seq, alpha):
    m, d = x2.shape
    tm = min(1024, seq)
    per_seq = seq // tm
    w_g2 = jnp.concatenate([w_gm[:, half * d + c * MERGE_TILE:half * d + (c + 1) * MERGE_TILE]
                            for c in range(d // MERGE_TILE) for half in range(2)], axis=1)
    resident = lambda a: pl.BlockSpec(a.shape, lambda i: (0,) * a.ndim, pipeline_mode=pl.Buffered(1))
    kern = functools.partial(_merge_kernel, alpha=alpha)
    return pl.pallas_call(
        kern,
        grid=(m // tm,),
        in_specs=[pl.BlockSpec((tm, d), lambda i: (i, 0)),
                  pl.BlockSpec((1, 6, d), lambda i: (i // per_seq, 0, 0)),
                  pl.BlockSpec((tm, o_nsa.shape[1]), lambda i: (i, 0)),
                  pl.BlockSpec((tm, o_diff.shape[1]), lambda i: (i, 0)),
                  resident(w_g2), resident(w_bn), resident(w_bd), resident(w_o),
                  pl.BlockSpec((1, d), lambda i: (0, 0)),
                  pl.BlockSpec((1, d), lambda i: (0, 0))],
        out_specs=pl.BlockSpec((tm, d), lambda i: (i, 0)),
        out_shape=jax.ShapeDtypeStruct((m, d), F32),
        compiler_params=pltpu.CompilerParams(dimension_semantics=("arbitrary",),
                                             vmem_limit_bytes=VMEM_LIMIT),
        name="merge",
    )(x2, ada3, o_nsa, o_diff, w_g2, w_bn, w_bd, w_o, ln_g.reshape(1, d), ln_b.reshape(1, d))


FFN_TILE = 256


def _ffn_kernel(x_ref, ada_ref, wab_ref, wo_ref, g_ref, b_ref, o_ref, *, alpha):
    x = x_ref[...]
    sh = ada_ref[0, 3:4, :]
    sc = ada_ref[0, 4:5, :]
    gate = ada_ref[0, 5:6, :]
    u = (x * (1.0 + sc) + sh).astype(BF16)
    acc = jnp.zeros(x.shape, F32)
    for c in range(wo_ref.shape[0] // FFN_TILE):
        ab = jnp.dot(u, wab_ref[:, 2 * c * FFN_TILE:2 * (c + 1) * FFN_TILE], preferred_element_type=F32)
        a = ab[:, 0:FFN_TILE]
        h = ((a * jax.nn.sigmoid(a)) * ab[:, FFN_TILE:2 * FFN_TILE]).astype(BF16)
        acc = acc + jnp.dot(h, wo_ref[c * FFN_TILE:(c + 1) * FFN_TILE, :], preferred_element_type=F32)
    o_ref[...] = _layer_norm(alpha * x + (1.0 + gate) * acc, g_ref[...], b_ref[...])


def _ffn(x2, ada3, w_in, w_out, ln_g, ln_b, seq, alpha):
    m, d = x2.shape
    dff = w_out.shape[0]
    tm = min(1024, seq)
    per_seq = seq // tm
    nt = dff // FFN_TILE
    w_ab = jnp.concatenate([w_in[:, half * dff + c * FFN_TILE:half * dff + (c + 1) * FFN_TILE]
                            for c in range(nt) for half in range(2)], axis=1)
    resident = lambda a: pl.BlockSpec(a.shape, lambda i: (0,) * a.ndim, pipeline_mode=pl.Buffered(1))
    kern = functools.partial(_ffn_kernel, alpha=alpha)
    return pl.pallas_call(
        kern,
        grid=(m // tm,),
        in_specs=[pl.BlockSpec((tm, d), lambda i: (i, 0)),
                  pl.BlockSpec((1, 6, d), lambda i: (i // per_seq, 0, 0)),
                  resident(w_ab), resident(w_out),
                  pl.BlockSpec((1, d), lambda i: (0, 0)),
                  pl.BlockSpec((1, d), lambda i: (0, 0))],
        out_specs=pl.BlockSpec((tm, d), lambda i: (i, 0)),
        out_shape=jax.ShapeDtypeStruct((m, d), F32),
        compiler_params=pltpu.CompilerParams(dimension_semantics=("arbitrary",),
                                             vmem_limit_bytes=VMEM_LIMIT),
        name="ffn",
    )(x2, ada3, w_ab, w_out, ln_g.reshape(1, d), ln_b.reshape(1, d))


def _rope_tables(seq):
    half = ROPE_DIMS // 2
    inv_freq = ROPE_THETA ** (-jnp.arange(half, dtype=F32) * 2.0 / ROPE_DIMS)
    ang = jnp.arange(seq, dtype=F32)[:, None] * inv_freq[None, :]
    cos, sin = jnp.cos(ang), jnp.sin(ang)
    ones = jnp.ones((seq, HEAD_DIM - ROPE_DIMS), F32)
    zeros = jnp.zeros((seq, HEAD_DIM - ROPE_DIMS), F32)
    z8 = jnp.zeros((seq, half), F32)
    cos_h = jnp.concatenate([cos, cos, ones], axis=1)
    sa_h = jnp.concatenate([-sin, z8, zeros], axis=1)
    sb_h = jnp.concatenate([z8, sin, zeros], axis=1)
    rep = lambda t: jnp.concatenate([t] * (LANES // HEAD_DIM), axis=1)
    return rep(cos_h), rep(sa_h), rep(sb_h)


def _mask_tables(seq):
    nb = seq // SEL_BLOCK
    nc = (seq - CMP_BLOCK) // CMP_STRIDE + 1
    ncp = seq // CMP_STRIDE
    cs = np.arange(nc) * CMP_STRIDE
    bs = np.arange(nb) * SEL_BLOCK
    ov = np.minimum(cs[:, None] + CMP_BLOCK, bs[None, :] + SEL_BLOCK) - np.maximum(cs[:, None], bs[None, :])
    ov = np.clip(ov, 0, None) / CMP_BLOCK
    ovt = np.zeros((SEL_BLOCK, ncp), np.float32)
    ovt[:nb, :nc] = ov.T
    tl = np.arange(NSA_LANES) % SEL_BLOCK
    dtab = (np.arange(WIN_KEYS)[:, None] - tl[None, :]).astype(np.int32)
    ctab = (np.arange(ncp)[:, None] * CMP_STRIDE + (CMP_BLOCK - 1) - tl[None, :]).astype(np.int32)
    return jnp.asarray(ovt, BF16), jnp.asarray(dtab), jnp.asarray(ctab)


def _proj_weight(w_in_l):
    d = w_in_l.shape[0]
    sizes = (512, 128, 128, 128, 128, 128, 128, 24, 512, 512, 512, 2 * d)
    offs = np.cumsum((0,) + sizes)
    q_n, kc, vc, ks, vs, kw, vw, g_n, q_d, k_d, v_d, g_m = [w_in_l[:, offs[k]:offs[k + 1]] for k in range(12)]
    qh = q_n.reshape(d, NSA_HEADS, HEAD_DIM)
    z = jnp.zeros_like(qh)
    first = jnp.arange(NSA_HEADS)[None, :, None] < NSA_GROUP
    q_wide = jnp.concatenate([jnp.where(first, qh, z), jnp.where(first, z, qh)], axis=-1).reshape(d, NSA_HEADS * LANES)
    zeros = lambda n: jnp.zeros((d, n), w_in_l.dtype)
    half_slabs = [jnp.concatenate([t[:, g * HEAD_DIM:(g + 1) * HEAD_DIM], zeros(LANES - HEAD_DIM)], axis=1)
                  for t in (kc, vc) for g in range(NSA_KV_HEADS)]
    w = jnp.concatenate([q_wide, q_wide, q_d, k_d, ks, kw, g_n, zeros(COL_KV - COL_GN - g_n.shape[1])]
                        + half_slabs, axis=1)
    assert w.shape[1] == PROJ_COLS
    wtv = jnp.concatenate([v_d, vs, vw], axis=1).T
    scale = np.ones((1, PROJ_COLS), np.float32)
    scale[:, COL_QR:COL_QD + 512] = QK_SCALE
    return w.astype(BF16), wtv.astype(BF16), jnp.asarray(scale), g_m.astype(BF16)


def kernel(x, c, w_ada, b_ada, w_in, cmp_pe_k, cmp_w1_k, cmp_w2_k, cmp_pe_v, cmp_w1_v, cmp_w2_v, diff_lambda, diff_norm_g, w_branch_nsa, w_branch_diff, w_out, ln1_g, ln1_b, w_ffn_in, w_ffn_out, ln2_g, ln2_b):
    bsz, seq, d = x.shape
    depth = w_ada.shape[0]
    assert seq % 1024 == 0, seq
    assert seq // SEL_BLOCK <= SEL_BLOCK
    alpha = (2 * depth) ** 0.25
    ncp = seq // CMP_STRIDE
    nblk = seq // SEL_BLOCK
    cos_t, sa_t, sb_t = _rope_tables(seq)
    ovt, dtab, ctab = _mask_tables(seq)
    onehot = (jnp.arange(seq)[:, None] // SEL_BLOCK == jnp.arange(LANES)[None, :]).astype(BF16)

    x2 = x.reshape(bsz * seq, d)
    for l in range(depth):
        lambda_init = 0.8 - 0.6 * math.exp(-0.3 * l)
        ada3 = _ada(c, w_ada[l], b_ada[l]).reshape(bsz, 6, d)
        w_proj, wtv, colscale, w_gm = _proj_weight(w_in[l])
        proj, gates, vt512, vw128, *kv_raw = _proj(x2, ada3, w_proj, wtv, colscale, cos_t, sa_t, sb_t, seq)
        proj3 = proj.reshape(bsz, seq, PROJ_COLS)
        chunks = [t.reshape(bsz, ncp, CMP_STRIDE * HEAD_DIM) for t in kv_raw]
        w1 = jnp.stack([cmp_w1_k[l], cmp_w1_v[l]]).astype(BF16)
        pe = jnp.stack([cmp_pe_k[l], cmp_pe_v[l]]).reshape(2, 1, CMP_BLOCK * HEAD_DIM)
        pe = jnp.broadcast_to(pe, (2, 8, CMP_BLOCK * HEAD_DIM)).astype(BF16)
        w2k = cmp_w2_k[l].astype(BF16)
        w2v = cmp_w2_v[l].astype(BF16).T
        w2k = jnp.stack([jnp.concatenate([w2k, jnp.zeros_like(w2k)], axis=1),
                         jnp.concatenate([jnp.zeros_like(w2k), w2k], axis=1)])
        w2vt = jnp.stack([jnp.concatenate([w2v, jnp.zeros_like(w2v)], axis=0),
                          jnp.concatenate([jnp.zeros_like(w2v), w2v], axis=0)])
        kcmp, vct = _compress(chunks, w1, pe, w2k, w2vt)

        nstep = nblk // NSA_PAIR
        gates_t = gates.reshape(bsz, nstep, NSA_PAIR, SEL_BLOCK, LANES)[..., :NSA_HEADS * 3]
        gates_t = gates_t.reshape(bsz, nstep, NSA_PAIR, SEL_BLOCK, NSA_KV_HEADS, NSA_GROUP, 3)
        gates_t = gates_t.transpose(0, 1, 6, 4, 2, 5, 3).reshape(bsz, nstep, 3, NSA_LANES)
        gates_t = jnp.pad(gates_t, ((0, 0), (0, 0), (0, 8 - 3), (0, 0)))
        o_nsa = _nsa(proj3, gates_t, onehot, vt512, vw128, kcmp, vct, ovt, dtab, ctab)
        o_diff = _diff(proj3, vt512, diff_lambda[l], diff_norm_g[l], lambda_init)

        x2 = _merge(x2, ada3, o_nsa.reshape(bsz * seq, -1), o_diff.reshape(bsz * seq, -1), w_gm,
                    w_branch_nsa[l].astype(BF16), w_branch_diff[l].astype(BF16), w_out[l].astype(BF16),
                    ln1_g[l], ln1_b[l], seq, alpha)
        x2 = _ffn(x2, ada3, w_ffn_in[l].astype(BF16), w_ffn_out[l].astype(BF16), ln2_g[l], ln2_b[l], seq, alpha)
    return x2.reshape(bsz, seq, d)
```
